```python
import jax, jax.numpy as jnp
from jax import lax
import numpy as np

D_MODEL = 1024
BATCH = 8
SEQ = 2048
DEPTH = 4
DEC_BATCH = 128
DEC_SEQ = 1
PAST_LEN = 8192
PAGE_SIZE = 128

N_MIXERS = 3
N_MLA = (DEPTH + 2) // 3
N_CMLP = (DEPTH + 1) // 3
N_POOLMIX = DEPTH // 3
DEEPNORM_ALPHA = (2 * DEPTH) ** 0.25
DEEPNORM_BETA = (8 * DEPTH) ** -0.25
LN_EPS = 1e-5
RMS_EPS = 1e-6
NEG_INF = -1e30

MLA_HEADS = D_MODEL // 128
NOPE_DIM = 128
ROPE_DIM = 64
V_DIM = 128
Q_RANK = 3 * D_MODEL // 8
KV_RANK = D_MODEL // 4
ROPE_THETA = 10000.0
Q_BLOCK = 128
ATTN_SCALE = (NOPE_DIM + ROPE_DIM) ** -0.5

CHUNK = 128
CMLP_HEADS = 8
D_CMLP = 3 * D_MODEL
CMLP_HEAD_DIM = D_CMLP // CMLP_HEADS

POOL_WINDOWS = (2, 4, 8, 16)
POOL_GROUPS = 4
D_POOL = D_MODEL
POOL_GROUP_DIM = D_POOL // POOL_GROUPS
POOL_BUF = 15

D_FF = 2688

kernel_name = "hybrid_mla_chunkmlp_pool_macaron_step"


def _layernorm(x, g, b):
    xf = x.astype(jnp.float32)
    mu = jnp.mean(xf, axis=-1, keepdims=True)
    var = jnp.mean(jnp.square(xf - mu), axis=-1, keepdims=True)
    return ((xf - mu) * lax.rsqrt(var + LN_EPS) * g + b).astype(x.dtype)


def _rmsnorm(x, g):
    xf = x.astype(jnp.float32)
    return (xf * lax.rsqrt(jnp.mean(jnp.square(xf), axis=-1, keepdims=True) + RMS_EPS) * g).astype(x.dtype)


def _post(x, delta, g, b):
    return _layernorm(DEEPNORM_ALPHA * x + delta, g, b)


def _swiglu(x, wg, wu, wd):
    return (jax.nn.silu(x @ wg) * (x @ wu)) @ wd


def _rope(x, pos):
    half = x.shape[-1] // 2
    inv = ROPE_THETA ** (-jnp.arange(half, dtype=jnp.float32) / half)
    ang = pos.astype(jnp.float32)[:, None] * inv[None, :]
    cos = jnp.cos(ang)[None, :, None, :]
    sin = jnp.sin(ang)[None, :, None, :]
    xf = x.astype(jnp.float32)
    x1, x2 = xf[..., :half], xf[..., half:]
    return jnp.concatenate([x1 * cos - x2 * sin, x1 * sin + x2 * cos], axis=-1).astype(x.dtype)


def _mla_project(x, pos, w_in, q_norm, kv_norm, w_uq, w_uk):
    B, S, _ = x.shape
    lat = x @ w_in
    c_q = _rmsnorm(lat[..., :Q_RANK], q_norm)
    c_kv = _rmsnorm(lat[..., Q_RANK:Q_RANK + KV_RANK], kv_norm)
    k_rope = _rope(lat[..., Q_RANK + KV_RANK:][:, :, None, :], pos)[:, :, 0]
    q = (c_q @ w_uq).reshape(B, S, MLA_HEADS, NOPE_DIM + ROPE_DIM)
    q_rope = _rope(q[..., NOPE_DIM:], pos)
    q_lat = jnp.einsum('bshd,chd->bshc', q[..., :NOPE_DIM], w_uk)
    return q_lat, q_rope, c_kv, k_rope


def _mla_output(out_lat, w_uv, w_o):
    B, S = out_lat.shape[:2]
    o = jnp.einsum('bshc,chv->bshv', out_lat, w_uv).reshape(B, S, MLA_HEADS * V_DIM)
    return o @ w_o


def _prompt_attention(q_lat, q_rope, c_kv, k_rope):
    B, S, H, C = q_lat.shape
    nb = S // Q_BLOCK
    ql = q_lat.reshape(B, nb, Q_BLOCK, H, C).transpose(1, 0, 2, 3, 4)
    qr = q_rope.reshape(B, nb, Q_BLOCK, H, ROPE_DIM).transpose(1, 0, 2, 3, 4)
    kpos = jnp.arange(S)

    def block(args):
        qlb, qrb, bi = args
        s = jnp.einsum('bqhc,bkc->bhqk', qlb, c_kv) + jnp.einsum('bqhr,bkr->bhqk', qrb, k_rope)
        qpos = bi * Q_BLOCK + jnp.arange(Q_BLOCK)
        s = jnp.where(kpos[None, :] <= qpos[:, None], s.astype(jnp.float32) * ATTN_SCALE, NEG_INF)
        p = jax.nn.softmax(s, axis=-1).astype(c_kv.dtype)
        return jnp.einsum('bhqk,bkc->bqhc', p, c_kv)

    out = lax.map(block, (ql, qr, jnp.arange(nb)))
    return out.transpose(1, 0, 2, 3, 4).reshape(B, S, H, C)


def _sample_attention(q_lat, q_rope, c_kv, k_rope, past_ckv, past_kr):
    T = q_lat.shape[1]
    P = past_ckv.shape[1]
    s_past = jnp.einsum('bqhc,bkc->bhqk', q_lat, past_ckv) + jnp.einsum('bqhr,bkr->bhqk', q_rope, past_kr)
    s_new = jnp.einsum('bqhc,bkc->bhqk', q_lat, c_kv) + jnp.einsum('bqhr,bkr->bhqk', q_rope, k_rope)
    causal = jnp.tril(jnp.ones((T, T), dtype=bool))
    s_new = jnp.where(causal, s_new.astype(jnp.float32) * ATTN_SCALE, NEG_INF)
    s = jnp.concatenate([s_past.astype(jnp.float32) * ATTN_SCALE, s_new], axis=-1)
    p = jax.nn.softmax(s, axis=-1).astype(c_kv.dtype)
    return (jnp.einsum('bhqk,bkc->bqhc', p[..., :P], past_ckv)
            + jnp.einsum('bhqk,bkc->bqhc', p[..., P:], c_kv))


def _chunk_mlp(x, w_in, b_in, v_g, v_b, w_s, b_s, w_out):
    B, S, _ = x.shape
    pad = (-S) % CHUNK
    xp = jnp.pad(x, ((0, 0), (0, pad), (0, 0)))
    Sp = S + pad
    nc = Sp // CHUNK
    z = jax.nn.gelu(xp @ w_in + b_in)
    u, v = z[..., :D_CMLP], z[..., D_CMLP:]
    v = _layernorm(v, v_g, v_b)
    tril = jnp.tril(jnp.ones((CHUNK, CHUNK), dtype=w_s.dtype))
    vc = v.reshape(B, nc, CHUNK, CMLP_HEADS, CMLP_HEAD_DIM)
    mixed = jnp.einsum('hij,bnjhd->bnihd', w_s * tril, vc) + b_s.T[None, None, :, :, None]
    out = (u * mixed.reshape(B, Sp, D_CMLP)) @ w_out
    return out[:, :S], v[:, :S]


def _pool_tokens(h, scale, w_grp):
    B, L, _ = h.shape
    hf = h.astype(jnp.float32)
    cs = jnp.concatenate([jnp.zeros((B, 1, D_POOL), jnp.float32), jnp.cumsum(hf, axis=1)], axis=1)
    hi = jnp.arange(L) + 1
    outs = []
    for g, w in enumerate(POOL_WINDOWS):
        lo = jnp.maximum(hi - w, 0)
        sl = slice(g * POOL_GROUP_DIM, (g + 1) * POOL_GROUP_DIM)
        csg = cs[..., sl]
        mean = (csg[:, hi] - csg[:, lo]) / (hi - lo).astype(jnp.float32)[None, :, None]
        outs.append(mean - hf[..., sl])
    pooled = jnp.stack(outs, axis=2).astype(h.dtype)
    z = jnp.einsum('blgd,gde->blge', pooled, w_grp).reshape(B, L, D_POOL)
    return z * scale


def setup_inputs(seed: int = 0) -> dict:
    key = jax.random.key(seed)
    ks = iter(jax.random.split(key, 48))
    f32 = jnp.float32

    def nrm(shape, fan_in, mult=1.0):
        return jax.random.normal(next(ks), shape, f32) * (mult * fan_in ** -0.5)

    def near_one(shape):
        return 1.0 + 0.05 * jax.random.normal(next(ks), shape, f32)

    def small(shape):
        return 0.02 * jax.random.normal(next(ks), shape, f32)

    n_pages = PAST_LEN // PAGE_SIZE
    n_used = DEC_BATCH * n_pages
    n_pool = n_used + n_used // 4
    inp = {}
    inp['x_prompt'] = jax.random.normal(next(ks), (BATCH, SEQ, D_MODEL), f32)
    inp['x_sample'] = jax.random.normal(next(ks), (DEC_BATCH, DEC_SEQ, D_MODEL), f32)
    inp['cache_kv_latent'] = jax.random.normal(next(ks), (N_MLA, n_pool, PAGE_SIZE, KV_RANK), f32)
    inp['cache_k_rope'] = jax.random.normal(next(ks), (N_MLA, n_pool, PAGE_SIZE, ROPE_DIM), f32)
    inp['state_pool'] = jax.random.normal(next(ks), (N_POOLMIX, DEC_BATCH, POOL_BUF, D_POOL), f32)
    inp['page_table'] = jax.random.permutation(next(ks), n_pool)[:n_used].reshape(DEC_BATCH, n_pages).astype(jnp.int32)
    inp['ln_g'] = near_one((DEPTH, 3, D_MODEL))
    inp['ln_b'] = small((DEPTH, 3, D_MODEL))
    inp['ffn_w_gate'] = nrm((DEPTH, 2, D_MODEL, D_FF), D_MODEL)
    inp['ffn_w_up'] = nrm((DEPTH, 2, D_MODEL, D_FF), D_MODEL)
    inp['ffn_w_down'] = nrm((DEPTH, 2, D_FF, D_MODEL), D_FF, DEEPNORM_BETA)
    inp['a_w_in'] = nrm((N_MLA, D_MODEL, Q_RANK + KV_RANK + ROPE_DIM), D_MODEL)
    inp['a_q_norm'] = near_one((N_MLA, Q_RANK))
    inp['a_kv_norm'] = near_one((N_MLA, KV_RANK))
    inp['a_w_uq'] = nrm((N_MLA, Q_RANK, MLA_HEADS * (NOPE_DIM + ROPE_DIM)), Q_RANK)
    inp['a_w_uk'] = nrm((N_MLA, KV_RANK, MLA_HEADS, NOPE_DIM), KV_RANK)
    inp['a_w_uv'] = nrm((N_MLA, KV_RANK, MLA_HEADS, V_DIM), KV_RANK)
    inp['a_w_o'] = nrm((N_MLA, MLA_HEADS * V_DIM, D_MODEL), MLA_HEADS * V_DIM, DEEPNORM_BETA)
    inp['b_w_in'] = nrm((N_CMLP, D_MODEL, 2 * D_CMLP), D_MODEL)
    inp['b_b_in'] = small((N_CMLP, 2 * D_CMLP))
    inp['b_v_norm_g'] = near_one((N_CMLP, D_CMLP))
    inp['b_v_norm_b'] = small((N_CMLP, D_CMLP))
    inp['b_w_s'] = nrm((N_CMLP, CMLP_HEADS, CHUNK, CHUNK), CHUNK)
    inp['b_b_s'] = near_one((N_CMLP, CMLP_HEADS, CHUNK))
    inp['b_w_out'] = nrm((N_CMLP, D_CMLP, D_MODEL), D_CMLP, DEEPNORM_BETA)
    inp['c_w_in'] = nrm((N_POOLMIX, D_MODEL, D_POOL), D_MODEL)
    inp['c_w_grp'] = nrm((N_POOLMIX, POOL_GROUPS, POOL_GROUP_DIM, POOL_GROUP_DIM), POOL_GROUP_DIM)
    inp['c_scale'] = near_one((N_POOLMIX, D_POOL))
    inp['c_w_out'] = nrm((N_POOLMIX, D_POOL, D_MODEL), D_POOL, DEEPNORM_BETA)
    return inp


def reference(x_prompt, x_sample, cache_kv_latent, cache_k_rope, state_pool, page_table,
              ln_g, ln_b, ffn_w_gate, ffn_w_up, ffn_w_down,
              a_w_in, a_q_norm, a_kv_norm, a_w_uq, a_w_uk, a_w_uv, a_w_o,
              b_w_in, b_b_in, b_v_norm_g, b_v_norm_b, b_w_s, b_b_s, b_w_out,
              c_w_in, c_w_grp, c_scale, c_w_out):
    S_p = x_prompt.shape[1]
    DB, T = x_sample.shape[:2]
    past_len = page_table.shape[1] * PAGE_SIZE
    pos_p = jnp.arange(S_p)
    pos_s = past_len + jnp.arange(T)
    y_p, y_s = x_prompt, x_sample
    ckv_p, kr_p, ckv_s, kr_s, v_s_rows, pool_p, pool_s = [], [], [], [], [], [], []

    for i in range(DEPTH):
        kind = i % N_MIXERS
        j = i // N_MIXERS
        y_p = _post(y_p, 0.5 * _swiglu(y_p, ffn_w_gate[i, 0], ffn_w_up[i, 0], ffn_w_down[i, 0]), ln_g[i, 0], ln_b[i, 0])
        y_s = _post(y_s, 0.5 * _swiglu(y_s, ffn_w_gate[i, 0], ffn_w_up[i, 0], ffn_w_down[i, 0]), ln_g[i, 0], ln_b[i, 0])

        if kind == 0:
            ql, qr, ckv, kr = _mla_project(y_p, pos_p, a_w_in[j], a_q_norm[j], a_kv_norm[j], a_w_uq[j], a_w_uk[j])
            mix_p = _mla_output(_prompt_attention(ql, qr, ckv, kr), a_w_uv[j], a_w_o[j])
            ckv_p.append(ckv)
            kr_p.append(kr)
            ql, qr, ckv, kr = _mla_project(y_s, pos_s, a_w_in[j], a_q_norm[j], a_kv_norm[j], a_w_uq[j], a_w_uk[j])
            past_ckv = cache_kv_latent[j, page_table].reshape(DB, past_len, KV_RANK)
            past_kr = cache_k_rope[j, page_table].reshape(DB, past_len, ROPE_DIM)
            mix_s = _mla_output(_sample_attention(ql, qr, ckv, kr, past_ckv, past_kr), a_w_uv[j], a_w_o[j])
            ckv_s.append(ckv)
            kr_s.append(kr)
        elif kind == 1:
            mix_p, _ = _chunk_mlp(y_p, b_w_in[j], b_b_in[j], b_v_norm_g[j], b_v_norm_b[j], b_w_s[j], b_b_s[j], b_w_out[j])
            mix_s, v_rows = _chunk_mlp(y_s, b_w_in[j], b_b_in[j], b_v_norm_g[j], b_v_norm_b[j], b_w_s[j], b_b_s[j], b_w_out[j])
            v_s_rows.append(v_rows)
        else:
            h_p = y_p @ c_w_in[j]
            mix_p = _pool_tokens(h_p, c_scale[j], c_w_grp[j]) @ c_w_out[j]
            pool_p.append(h_p[:, -POOL_BUF:])
            h_s = jnp.concatenate([state_pool[j], y_s @ c_w_in[j]], axis=1)
            mix_s = _pool_tokens(h_s, c_scale[j], c_w_grp[j])[:, -T:] @ c_w_out[j]
            pool_s.append(h_s[:, -POOL_BUF:])

        y_p = _post(y_p, mix_p, ln_g[i, 1], ln_b[i, 1])
        y_s = _post(y_s, mix_s, ln_g[i, 1], ln_b[i, 1])
        y_p = _post(y_p, 0.5 * _swiglu(y_p, ffn_w_gate[i, 1], ffn_w_up[i, 1], ffn_w_down[i, 1]), ln_g[i, 2], ln_b[i, 2])
        y_s = _post(y_s, 0.5 * _swiglu(y_s, ffn_w_gate[i, 1], ffn_w_up[i, 1], ffn_w_down[i, 1]), ln_g[i, 2], ln_b[i, 2])

    new_kv_latent_prompt = jnp.stack(ckv_p)
    new_k_rope_prompt = jnp.stack(kr_p)
    new_kv_latent_sample = jnp.stack(ckv_s)
    new_k_rope_sample = jnp.stack(kr_s)
    new_chunk_v_sample = jnp.stack(v_s_rows)
    new_pool_prompt = jnp.stack(pool_p)
    new_pool_sample = jnp.stack(pool_s)
    return (y_p, y_s, new_kv_latent_prompt, new_k_rope_prompt, new_kv_latent_sample, new_k_rope_sample, new_chunk_v_sample, new_pool_prompt, new_pool_sample)
```

```python
import functools

import jax
import jax.numpy as jnp
from jax import lax
from jax.experimental import pallas as pl
from jax.experimental.pallas import tpu as pltpu

F32 = jnp.float32
BF16 = jnp.bfloat16

LN_EPS = 1e-5
RMS_EPS = 1e-6
NEG_INF = -1e30
ROPE_THETA = 10000.0
POOL_WINDOWS = (2, 4, 8, 16)
CHUNK = 128

V7X_VMEM_BYTES = 64 * 1024 * 1024
VMEM_LIMIT = V7X_VMEM_BYTES - 8 * 1024 * 1024
LANES = 128


def _params(n_grid_dims):
    return pltpu.CompilerParams(
        dimension_semantics=("arbitrary",) * n_grid_dims, vmem_limit_bytes=VMEM_LIMIT)


def _const_spec(shape):
    nd = len(shape)
    return pl.BlockSpec(shape, lambda *_: (0,) * nd, pipeline_mode=pl.Buffered(1))


def _dot(a, b):
    return jnp.dot(a, b, preferred_element_type=F32)


def _dot_nt(a, b):
    return lax.dot_general(a, b, (((1,), (1,)), ((), ())), preferred_element_type=F32)


def _layernorm(x, g, b):
    mu = jnp.mean(x, axis=-1, keepdims=True)
    xc = x - mu
    var = jnp.mean(xc * xc, axis=-1, keepdims=True)
    return xc * lax.rsqrt(var + LN_EPS) * g + b


def _rmsnorm(x, g):
    return x * lax.rsqrt(jnp.mean(x * x, axis=-1, keepdims=True) + RMS_EPS) * g


def _ffn_kernel(x_ref, wg_ref, wu_ref, wd_ref, g_ref, b_ref, o_ref, *, alpha, n_chunks):
    x = x_ref[...]
    xb = x.astype(BF16)
    fc = wg_ref.shape[1] // n_chunks
    acc = None
    for c in range(n_chunks):
        sl = slice(c * fc, (c + 1) * fc)
        gate = _dot(xb, wg_ref[:, sl])
        up = _dot(xb, wu_ref[:, sl])
        h = (gate * jax.nn.sigmoid(gate) * up).astype(BF16)
        d = _dot(h, wd_ref[sl, :])
        acc = d if acc is None else acc + d
    o_ref[...] = _layernorm(alpha * x + 0.5 * acc, g_ref[...], b_ref[...])


def _ffn_chunks(d_ff):
    for n in range(1, d_ff // LANES + 1):
        if d_ff % n == 0 and (d_ff // n) % LANES == 0 and d_ff // n <= 1024:
            return n
    return 1


def _ffn_post(x, wg, wu, wd, g, b, *, layer, half, alpha, tm):
    n, d = x.shape
    f = wg.shape[-1]
    row = pl.BlockSpec((tm, d), lambda i: (i, 0))

    def weight(k, m):
        return pl.BlockSpec((None, None, k, m), lambda i: (layer, half, 0, 0), pipeline_mode=pl.Buffered(1))

    return pl.pallas_call(
        functools.partial(_ffn_kernel, alpha=alpha, n_chunks=_ffn_chunks(f)),
        grid=(n // tm,),
        in_specs=[row, weight(d, f), weight(d, f), weight(f, d),
                  _const_spec((1, d)), _const_spec((1, d))],
        out_specs=row,
        out_shape=jax.ShapeDtypeStruct((n, d), F32),
        compiler_params=_params(1),
        name="ffn_post",
    )(x, wg, wu, wd, g, b)


def _mla_proj_kernel(x_ref, win_ref, qn_ref, kvn_ref, wuq_ref, wukt_ref, inv_ref, sgn_ref,
                     qlat_ref, qrope_ref, ckv_ref, kr_ref, ckvb_ref, krb_ref,
                     *, tm, pos0, seq_len, q_rank, kv_rank, rope_dim, nope_dim, scale):
    heads = qlat_ref.shape[0]
    xb = x_ref[...].astype(BF16)
    lat = _dot(xb, win_ref[...])
    c_q = _rmsnorm(lat[:, :q_rank], qn_ref[...])
    c_kv = _rmsnorm(lat[:, q_rank:q_rank + kv_rank], kvn_ref[...])
    o_r = q_rank + kv_rank
    kr_raw = lat[:, o_r:o_r + rope_dim]
    kr_swp = lat[:, o_r + LANES:o_r + LANES + rope_dim]

    if seq_len == 1:
        pos = jnp.full((1, 1), pos0, F32)
    else:
        row = pl.program_id(0) * tm + lax.broadcasted_iota(jnp.int32, (tm, 1), 0)
        pos = (pos0 + row % seq_len).astype(F32)
    ang = pos * inv_ref[...]
    cos2 = jnp.cos(ang)
    sin2 = jnp.sin(ang) * sgn_ref[...]

    k_rope = kr_raw * cos2[:, :rope_dim] + kr_swp * sin2[:, :rope_dim]
    ckv_ref[...] = c_kv
    kr_ref[...] = k_rope
    ckvb_ref[...] = c_kv.astype(BF16)
    krb_ref[...] = k_rope.astype(BF16)

    q = _dot(c_q.astype(BF16), wuq_ref[...])
    n_nope = heads * nope_dim
    n_rope = heads * rope_dim
    reps = n_rope // LANES
    cos_h = jnp.concatenate([cos2] * reps, axis=-1)
    sin_h = jnp.concatenate([sin2] * reps, axis=-1)
    q_rope = (q[:, n_nope:n_nope + n_rope] * cos_h + q[:, n_nope + n_rope:] * sin_h) * scale
    for h in range(heads):
        q_nope = q[:, h * nope_dim:(h + 1) * nope_dim].astype(BF16)
        qlat_ref[h] = (_dot(q_nope, wukt_ref[h]) * scale).astype(BF16)
        qrope_ref[h] = q_rope[:, h * rope_dim:(h + 1) * rope_dim].astype(BF16)


def _mla_weights(w_in, w_uq, w_uk, w_uv, w_o, *, q_rank, kv_rank, rope_dim, heads, nope_dim):
    half = rope_dim // 2
    d = w_in.shape[0]

    def swap(w):
        return jnp.concatenate([w[..., half:], w[..., :half]], axis=-1)

    w_r = w_in[:, q_rank + kv_rank:]
    zpad = jnp.zeros((d, LANES - rope_dim), w_in.dtype)
    w_in_x = jnp.concatenate([w_in[:, :q_rank + kv_rank], w_r, zpad, swap(w_r), zpad], axis=1)
    uq = w_uq.reshape(q_rank, heads, nope_dim + rope_dim)
    uq_n = uq[:, :, :nope_dim].reshape(q_rank, heads * nope_dim)
    uq_r = uq[:, :, nope_dim:]
    w_uq_x = jnp.concatenate([uq_n, uq_r.reshape(q_rank, heads * rope_dim),
                              swap(uq_r).reshape(q_rank, heads * rope_dim)], axis=1)
    w_uk_t = jnp.transpose(w_uk, (1, 2, 0))
    w_uv_h = jnp.transpose(w_uv, (1, 0, 2))
    return (w_in_x.astype(BF16), w_uq_x.astype(BF16), w_uk_t.astype(BF16),
            w_uv_h.astype(BF16), w_o.astype(BF16))


def _rope_tables(rope_dim):
    half = rope_dim // 2
    inv = ROPE_THETA ** (-jnp.arange(half, dtype=F32) / half)
    reps = LANES // half
    inv_l = jnp.tile(inv, reps)[None, :]
    sgn = jnp.tile(jnp.concatenate([-jnp.ones((half,), F32), jnp.ones((half,), F32)]), reps // 2)[None, :]
    return inv_l, sgn


def _mla_project(x, w_in_x, q_norm, kv_norm, w_uq_x, w_uk_t, *, tm, pos0, seq_len, rope_dim, scale):
    n, d = x.shape
    heads, nope_dim, kv_rank = w_uk_t.shape
    q_rank = w_uq_x.shape[0]
    inv_l, sgn = _rope_tables(rope_dim)
    row = lambda w: pl.BlockSpec((tm, w), lambda i: (i, 0))
    hrow = lambda w: pl.BlockSpec((heads, tm, w), lambda i: (0, i, 0))
    kern = functools.partial(
        _mla_proj_kernel, tm=tm, pos0=pos0, seq_len=seq_len, q_rank=q_rank, kv_rank=kv_rank,
        rope_dim=rope_dim, nope_dim=nope_dim, scale=scale)
    return pl.pallas_call(
        kern,
        grid=(n // tm,),
        in_specs=[row(d), _const_spec(w_in_x.shape), _const_spec((1, q_rank)), _const_spec((1, kv_rank)),
                  _const_spec(w_uq_x.shape), _const_spec(w_uk_t.shape),
                  _const_spec((1, LANES)), _const_spec((1, LANES))],
        out_specs=[hrow(kv_rank), hrow(rope_dim), row(kv_rank), row(rope_dim), row(kv_rank), row(rope_dim)],
        out_shape=[jax.ShapeDtypeStruct((heads, n, kv_rank), BF16),
                   jax.ShapeDtypeStruct((heads, n, rope_dim), BF16),
                   jax.ShapeDtypeStruct((n, kv_rank), F32),
                   jax.ShapeDtypeStruct((n, rope_dim), F32),
                   jax.ShapeDtypeStruct((n, kv_rank), BF16),
                   jax.ShapeDtypeStruct((n, rope_dim), BF16)],
        compiler_params=_params(1),
        name="mla_project",
    )(x, w_in_x, q_norm[None, :], kv_norm[None, :], w_uq_x, w_uk_t, inv_l, sgn)


def _prompt_attn_kernel(ql_ref, qr_ref, kc_ref, kr_ref, o_ref, m_sc, l_sc, acc_sc, *, tq, tk):
    heads, _, c = ql_ref.shape
    qi = pl.program_id(1)
    ql = ql_ref[...].reshape(heads * tq, c)
    qr = qr_ref[...].reshape(heads * tq, qr_ref.shape[2])
    m_sc[...] = jnp.full(m_sc.shape, NEG_INF, F32)
    l_sc[...] = jnp.zeros(l_sc.shape, F32)
    acc_sc[...] = jnp.zeros(acc_sc.shape, F32)

    def step(j, masked):
        k0 = pl.multiple_of(j * tk, tk)
        kc = kc_ref[pl.ds(k0, tk), :]
        kr = kr_ref[pl.ds(k0, tk), :]
        s = _dot_nt(ql, kc) + _dot_nt(qr, kr)
        if masked:
            q_pos = qi * tq + (lax.broadcasted_iota(jnp.int32, s.shape, 0) & (tq - 1))
            k_pos = k0 + lax.broadcasted_iota(jnp.int32, s.shape, 1)
            s = jnp.where(k_pos <= q_pos, s, NEG_INF)
        m_prev = m_sc[...]
        m_new = jnp.maximum(m_prev, jnp.max(s, axis=-1, keepdims=True))
        a = jnp.exp(m_prev - m_new)
        p = jnp.exp(s - m_new)
        l_sc[...] = a * l_sc[...] + jnp.sum(p, axis=-1, keepdims=True)
        acc_sc[...] = a * acc_sc[...] + _dot(p.astype(BF16), kc)
        m_sc[...] = m_new

    n_full = (qi * tq) // tk

    def body(j, carry):
        step(j, False)
        return carry

    lax.fori_loop(0, n_full, body, 0)
    step(n_full, True)

    out = acc_sc[...] / l_sc[...]
    for h in range(heads):
        o_ref[:, h * c:(h + 1) * c] = out[h * tq:(h + 1) * tq, :].astype(o_ref.dtype)


def _prompt_attention(q_lat, q_rope, kc, kr, *, batch, seq, tq, tk):
    heads, n, c = q_lat.shape
    r = q_rope.shape[2]
    assert tq & (tq - 1) == 0 and tk % tq == 0 and seq % tk == 0
    nq = seq // tq
    return pl.pallas_call(
        functools.partial(_prompt_attn_kernel, tq=tq, tk=tk),
        grid=(batch, nq),
        in_specs=[pl.BlockSpec((heads, tq, c), lambda b, i: (0, b * nq + i, 0)),
                  pl.BlockSpec((heads, tq, r), lambda b, i: (0, b * nq + i, 0)),
                  pl.BlockSpec((seq, c), lambda b, i: (b, 0)),
                  pl.BlockSpec((seq, r), lambda b, i: (b, 0))],
        out_specs=pl.BlockSpec((tq, heads * c), lambda b, i: (b * nq + i, 0)),
        out_shape=jax.ShapeDtypeStruct((n, heads * c), BF16),
        scratch_shapes=[pltpu.VMEM((heads * tq, 1), F32), pltpu.VMEM((heads * tq, 1), F32),
                        pltpu.VMEM((heads * tq, c), F32)],
        compiler_params=_params(2),
        name="prompt_attention",
    )(q_lat, q_rope, kc, kr)


def _sample_attn_kernel(pt_ref, ql_ref, qr_ref, cnew_ref, rnew_ref, ckv_hbm, kr_hbm, o_ref,
                        kbuf, rbuf, kb16, sem, *, layer, n_pages, page, n_chunks):
    b = pl.program_id(0)
    nb = pl.num_programs(0)
    slot = b % 2

    def page_copies(row, dst_slot, p):
        pg = pt_ref[row * n_pages + p]
        rows = pl.ds(p * page, page)
        return (pltpu.make_async_copy(ckv_hbm.at[layer, pg], kbuf.at[dst_slot, rows], sem.at[dst_slot, 0]),
                pltpu.make_async_copy(kr_hbm.at[layer, pg], rbuf.at[dst_slot, rows], sem.at[dst_slot, 1]))

    def start_row(row, dst_slot):
        for p in range(n_pages):
            for cp in page_copies(row, dst_slot, p):
                cp.start()

    @pl.when(b == 0)
    def _():
        start_row(0, 0)

    @pl.when(b + 1 < nb)
    def _():
        start_row(b + 1, 1 - slot)

    for p in range(n_pages):
        for cp in page_copies(b, slot, p):
            cp.wait()

    ql = ql_ref[0]
    qr = qr_ref[0]
    c_new = cnew_ref[0]
    r_new = rnew_ref[0]
    ck = (n_pages * page) // n_chunks
    s_parts = []
    for c in range(n_chunks):
        rows = pl.ds(c * ck, ck)
        kc = kbuf[slot, rows, :].astype(BF16)
        kb16[rows, :] = kc
        s_parts.append(_dot_nt(ql, kc) + _dot_nt(qr, rbuf[slot, rows, :].astype(BF16)))
    s_new = (jnp.sum(ql.astype(F32) * c_new, axis=-1, keepdims=True)
             + jnp.sum(qr.astype(F32) * r_new, axis=-1, keepdims=True))
    m = s_new
    for s in s_parts:
        m = jnp.maximum(m, jnp.max(s, axis=-1, keepdims=True))
    p_new = jnp.exp(s_new - m)
    l = p_new
    acc = p_new * c_new
    for c, s in enumerate(s_parts):
        p = jnp.exp(s - m)
        l = l + jnp.sum(p, axis=-1, keepdims=True)
        acc = acc + _dot(p.astype(BF16), kb16[pl.ds(c * ck, ck), :])
    o_ref[0] = (acc / l).astype(o_ref.dtype)


def _sample_attention(page_table, q_lat, q_rope, c_new, r_new, cache_kv, cache_kr, *, layer):
    db, heads, c = q_lat.shape
    r = q_rope.shape[2]
    n_pages = page_table.shape[1]
    page = cache_kv.shape[2]
    past = n_pages * page
    n_chunks = max(1, past // 2048)
    blk = lambda h, w: pl.BlockSpec((1, h, w), lambda b, pt: (b, 0, 0))
    grid_spec = pltpu.PrefetchScalarGridSpec(
        num_scalar_prefetch=1,
        grid=(db,),
        in_specs=[blk(heads, c), blk(heads, r), blk(1, c), blk(1, r),
                  pl.BlockSpec(memory_space=pl.ANY), pl.BlockSpec(memory_space=pl.ANY)],
        out_specs=blk(heads, c),
        scratch_shapes=[pltpu.VMEM((2, past, c), F32), pltpu.VMEM((2, past, r), F32),
                        pltpu.VMEM((past, c), BF16), pltpu.SemaphoreType.DMA((2, 2))],
    )
    return pl.pallas_call(
        functools.partial(_sample_attn_kernel, layer=layer, n_pages=n_pages, page=page, n_chunks=n_chunks),
        grid_spec=grid_spec,
        out_shape=jax.ShapeDtypeStruct((db, heads, c), BF16),
        compiler_params=_params(1),
        name="sample_attention",
    )(page_table.reshape(-1), q_lat, q_rope, c_new, r_new, cache_kv, cache_kr)


def _mla_out_kernel(o_ref, x_ref, wuv_ref, wo_ref, g_ref, b_ref, y_ref, *, alpha):
    heads, c, _ = wuv_ref.shape
    parts = [_dot(o_ref[:, h * c:(h + 1) * c], wuv_ref[h]).astype(BF16) for h in range(heads)]
    mix = _dot(jnp.concatenate(parts, axis=-1), wo_ref[...])
    y_ref[...] = _layernorm(alpha * x_ref[...] + mix, g_ref[...], b_ref[...])


def _mla_out_post(o_lat, x, w_uv_h, w_o, g, b, *, alpha, tm):
    n, d = x.shape
    row = lambda w: pl.BlockSpec((tm, w), lambda i: (i, 0))
    return pl.pallas_call(
        functools.partial(_mla_out_kernel, alpha=alpha),
        grid=(n // tm,),
        in_specs=[row(o_lat.shape[1]), row(d), _const_spec(w_uv_h.shape), _const_spec(w_o.shape),
                  _const_spec((1, d)), _const_spec((1, d))],
        out_specs=row(d),
        out_shape=jax.ShapeDtypeStruct((n, d), F32),
        compiler_params=_params(1),
        name="mla_out_post",
    )(o_lat, x, w_uv_h, w_o, g, b)


def _cmlp_kernel(*refs, alpha, tm, heads, sample):
    if sample:
        (x_ref, win_ref, bin_ref, vg_ref, vb_ref, wcol_ref, bcol_ref, wout_ref, g_ref, b_ref,
         y_ref, v_ref) = refs
    else:
        (x_ref, win_ref, bin_ref, vg_ref, vb_ref, ws_ref, bs_ref, wout_ref, g_ref, b_ref,
         y_ref, vn_sc) = refs
    dc = wout_ref.shape[0]
    hd = dc // heads
    x = x_ref[...]
    xb = x.astype(BF16)
    v = jax.nn.gelu(_dot(xb, win_ref[:, dc:]) + bin_ref[:, dc:])
    vn = _layernorm(v, vg_ref[...], vb_ref[...])
    if sample:
        v_ref[...] = vn
        mixed = vn * wcol_ref[...] + bcol_ref[...]
        u = jax.nn.gelu(_dot(xb, win_ref[:, :dc]) + bin_ref[:, :dc])
        mix = _dot((u * mixed).astype(BF16), wout_ref[...])
    else:
        vn_sc[...] = vn.astype(BF16)
        ri = lax.broadcasted_iota(jnp.int32, (CHUNK, CHUNK), 0)
        ci = lax.broadcasted_iota(jnp.int32, (CHUNK, CHUNK), 1)
        mix = None
        for h in range(heads):
            cols = slice(h * hd, (h + 1) * hd)
            w_tril = jnp.where(ci <= ri, ws_ref[h], 0.0).astype(BF16)
            bias = bs_ref[h]
            mixed = jnp.concatenate(
                [_dot(w_tril, vn_sc[c * CHUNK:(c + 1) * CHUNK, cols]) + bias for c in range(tm // CHUNK)],
                axis=0)
            u = jax.nn.gelu(_dot(xb, win_ref[:, cols]) + bin_ref[:, cols])
            d = _dot((u * mixed).astype(BF16), wout_ref[cols, :])
            mix = d if mix is None else mix + d
    y_ref[...] = _layernorm(alpha * x + mix, g_ref[...], b_ref[...])


def _chunk_mlp_post(x, w_in, b_in, v_g, v_b, w_s, b_s, w_out, g, b, *, alpha, tm, sample):
    n, d = x.shape
    dc = w_out.shape[0]
    heads = w_s.shape[0]
    row = pl.BlockSpec((tm, d), lambda i: (i, 0))
    common = [row, _const_spec(w_in.shape), _const_spec((1, 2 * dc)), _const_spec((1, dc)), _const_spec((1, dc))]
    tail = [_const_spec(w_out.shape), _const_spec((1, d)), _const_spec((1, d))]
    kern = functools.partial(_cmlp_kernel, alpha=alpha, tm=tm, heads=heads, sample=sample)
    y_shape = jax.ShapeDtypeStruct((n, d), F32)
    if sample:
        hd = dc // heads
        wcol = jnp.repeat(w_s[:, 0, 0], hd)[None, :]
        bcol = jnp.repeat(b_s[:, 0], hd)[None, :]
        return pl.pallas_call(
            kern, grid=(n // tm,),
            in_specs=common + [_const_spec((1, dc)), _const_spec((1, dc))] + tail,
            out_specs=[row, pl.BlockSpec((tm, dc), lambda i: (i, 0))],
            out_shape=[y_shape, jax.ShapeDtypeStruct((n, dc), F32)],
            compiler_params=_params(1), name="chunk_mlp_sample",
        )(x, w_in, b_in[None, :], v_g[None, :], v_b[None, :], wcol, bcol, w_out, g, b)
    assert tm % CHUNK == 0
    y = pl.pallas_call(
        kern, grid=(n // tm,),
        in_specs=common + [_const_spec(w_s.shape), _const_spec((heads, CHUNK, 1))] + tail,
        out_specs=row, out_shape=y_shape,
        scratch_shapes=[pltpu.VMEM((tm, dc), BF16)],
        compiler_params=_params(1), name="chunk_mlp_prompt",
    )(x, w_in, b_in[None, :], v_g[None, :], v_b[None, :], w_s, b_s[:, :, None], w_out, g, b)
    return y, None


def _pool_tail(pooled_groups, x, wgrp_ref, scale_ref, wout_ref, g_ref, b_ref, alpha):
    z = jnp.concatenate(
        [_dot(p.astype(BF16), wgrp_ref[gi]) for gi, p in enumerate(pooled_groups)], axis=-1)
    mix = _dot((z * scale_ref[...]).astype(BF16), wout_ref[...])
    return _layernorm(alpha * x + mix, g_ref[...], b_ref[...])


def _pool_prompt_kernel(x_ref, halo_ref, win_ref, wgrp_ref, scale_ref, wout_ref, g_ref, b_ref,
                        y_ref, last_ref, h_sc, *, alpha, tm, tiles_per_seq, halo):
    t0 = (pl.program_id(0) % tiles_per_seq) * tm
    x = x_ref[...]
    h = _dot(x.astype(BF16), win_ref[...])
    h_halo = _dot(halo_ref[...].astype(BF16), win_ref[...])
    h_sc[:halo, :] = jnp.where(t0 > 0, h_halo, 0.0)
    h_sc[halo:, :] = h
    last_ref[0] = h[tm - halo:, :]
    gd = h.shape[1] // len(POOL_WINDOWS)
    t = t0 + lax.broadcasted_iota(jnp.int32, (tm, 1), 0)
    pooled = []
    for gi, w in enumerate(POOL_WINDOWS):
        cols = slice(gi * gd, (gi + 1) * gd)
        tot = h[:, cols]
        for k in range(1, w):
            tot = tot + h_sc[halo - k:halo - k + tm, cols]
        cnt = jnp.minimum(t + 1, w).astype(F32)
        pooled.append(tot / cnt - h[:, cols])
    y_ref[...] = _pool_tail(pooled, x, wgrp_ref, scale_ref, wout_ref, g_ref, b_ref, alpha)


def _pool_prompt_post(x, w_in, w_grp, scale, w_out, g, b, *, alpha, tm, seq):
    n, d = x.shape
    dp = w_in.shape[1]
    halo = max(POOL_WINDOWS)
    assert seq % tm == 0 and tm % halo == 0 and halo % 8 == 0
    tiles_per_seq = seq // tm
    hb = tm // halo
    row = pl.BlockSpec((tm, d), lambda i: (i, 0))
    kern = functools.partial(_pool_prompt_kernel, alpha=alpha, tm=tm, tiles_per_seq=tiles_per_seq, halo=halo)
    return pl.pallas_call(
        kern, grid=(n // tm,),
        in_specs=[row, pl.BlockSpec((halo, d), lambda i: (jnp.maximum(i * hb - 1, 0), 0)),
                  _const_spec(w_in.shape), _const_spec(w_grp.shape), _const_spec((1, dp)),
                  _const_spec(w_out.shape), _const_spec((1, d)), _const_spec((1, d))],
        out_specs=[row, pl.BlockSpec((1, halo, dp), lambda i: (i // tiles_per_seq, 0, 0))],
        out_shape=[jax.ShapeDtypeStruct((n, d), F32),
                   jax.ShapeDtypeStruct((n // seq, halo, dp), F32)],
        scratch_shapes=[pltpu.VMEM((tm + halo, dp), F32)],
        compiler_params=_params(1), name="pool_prompt",
    )(x, x, w_in, w_grp, scale[None, :], w_out, g, b)


def _pool_sample_kernel(x_ref, st_ref, win_ref, wgrp_ref, scale_ref, wout_ref, g_ref, b_ref,
                        y_ref, h_ref, *, alpha):
    x = x_ref[...]
    h = _dot(x.astype(BF16), win_ref[...])
    h_ref[...] = h
    nbuf = st_ref.shape[0]
    gd = h.shape[1] // len(POOL_WINDOWS)
    pooled = []
    for gi, w in enumerate(POOL_WINDOWS):
        cols = slice(gi * gd, (gi + 1) * gd)
        tot = h[:, cols]
        for k in range(1, w):
            tot = tot + st_ref[nbuf - k, :, cols]
        pooled.append(tot / float(w) - h[:, cols])
    y_ref[...] = _pool_tail(pooled, x, wgrp_ref, scale_ref, wout_ref, g_ref, b_ref, alpha)


def _pool_sample_post(x, state, w_in, w_grp, scale, w_out, g, b, *, alpha):
    n, d = x.shape
    dp = w_in.shape[1]
    assert state.shape[0] == max(POOL_WINDOWS) - 1
    return pl.pallas_call(
        functools.partial(_pool_sample_kernel, alpha=alpha), grid=(1,),
        in_specs=[_const_spec((n, d)), _const_spec(state.shape), _const_spec(w_in.shape),
                  _const_spec(w_grp.shape), _const_spec((1, dp)), _const_spec(w_out.shape),
                  _const_spec((1, d)), _const_spec((1, d))],
        out_specs=[pl.BlockSpec((n, d), lambda i: (0, 0)), pl.BlockSpec((n, dp), lambda i: (0, 0))],
        out_shape=[jax.ShapeDtypeStruct((n, d), F32), jax.ShapeDtypeStruct((n, dp), F32)],
        compiler_params=_params(1), name="pool_sample",
    )(x, state, w_in, w_grp, scale[None, :], w_out, g, b)


def _row_tile(n, want):
    tm = min(n, want)
    assert n % tm == 0
    return tm


def kernel(x_prompt, x_sample, cache_kv_latent, cache_k_rope, state_pool, page_table, ln_g, ln_b, ffn_w_gate, ffn_w_up, ffn_w_down, a_w_in, a_q_norm, a_kv_norm, a_w_uq, a_w_uk, a_w_uv, a_w_o, b_w_in, b_b_in, b_v_norm_g, b_v_norm_b, b_w_s, b_b_s, b_w_out, c_w_in, c_w_grp, c_scale, c_w_out):
    batch, seq, d = x_prompt.shape
    db, t_new, _ = x_sample.shape
    assert t_new == 1, "decode attention handles one new token per sample row"
    depth = ln_g.shape[0]
    alpha = (2 * depth) ** 0.25
    kv_rank, heads, nope_dim = a_w_uk.shape[1:]
    rope_dim = cache_k_rope.shape[-1]
    q_rank = a_w_uq.shape[1]
    scale = (nope_dim + rope_dim) ** -0.5
    past_len = page_table.shape[1] * cache_kv_latent.shape[2]
    cmlp_heads = b_w_s.shape[1]
    n_p, n_s = batch * seq, db * t_new
    tm_p, tm_s = _row_tile(n_p, 512), _row_tile(n_s, 512)

    y_p = x_prompt.reshape(n_p, d)
    y_s = x_sample.reshape(n_s, d)
    wg, wu, wd = ffn_w_gate.astype(BF16), ffn_w_up.astype(BF16), ffn_w_down.astype(BF16)
    outs = {k: [] for k in ("ckv_p", "kr_p", "ckv_s", "kr_s", "v_s", "pool_p", "pool_s")}

    def ffn(y, tm, i, k, ln_idx):
        return _ffn_post(y, wg, wu, wd, ln_g[i, ln_idx][None, :], ln_b[i, ln_idx][None, :],
                         layer=i, half=k, alpha=alpha, tm=tm)

    for i in range(depth):
        kind, j = i % 3, i // 3
        y_p = ffn(y_p, tm_p, i, 0, 0)
        y_s = ffn(y_s, tm_s, i, 0, 0)
        g1, b1 = ln_g[i, 1][None, :], ln_b[i, 1][None, :]

        if kind == 0:
            w_in_x, w_uq_x, w_uk_t, w_uv_h, w_o = _mla_weights(
                a_w_in[j], a_w_uq[j], a_w_uk[j], a_w_uv[j], a_w_o[j],
                q_rank=q_rank, kv_rank=kv_rank, rope_dim=rope_dim, heads=heads, nope_dim=nope_dim)
            proj = functools.partial(_mla_project, w_in_x=w_in_x, q_norm=a_q_norm[j], kv_norm=a_kv_norm[j],
                                     w_uq_x=w_uq_x, w_uk_t=w_uk_t, rope_dim=rope_dim, scale=scale)
            ql, qr, ckv, kr, ckv_b, kr_b = proj(y_p, tm=tm_p, pos0=0, seq_len=seq)
            o_lat = _prompt_attention(ql, qr, ckv_b, kr_b, batch=batch, seq=seq, tq=128, tk=256)
            y_p = _mla_out_post(o_lat, y_p, w_uv_h, w_o, g1, b1, alpha=alpha, tm=tm_p)
            outs["ckv_p"].append(ckv.reshape(batch, seq, kv_rank))
            outs["kr_p"].append(kr.reshape(batch, seq, rope_dim))

            ql, qr, ckv, kr, _, _ = proj(y_s, tm=tm_s, pos0=past_len, seq_len=t_new)
            o_s = _sample_attention(page_table, jnp.transpose(ql, (1, 0, 2)), jnp.transpose(qr, (1, 0, 2)),
                                    ckv[:, None, :], kr[:, None, :], cache_kv_latent, cache_k_rope, layer=j)
            y_s = _mla_out_post(o_s.reshape(n_s, heads * kv_rank), y_s, w_uv_h, w_o, g1, b1, alpha=alpha, tm=tm_s)
            outs["ckv_s"].append(ckv.reshape(db, t_new, kv_rank))
            outs["kr_s"].append(kr.reshape(db, t_new, rope_dim))
        elif kind == 1:
            cm = functools.partial(_chunk_mlp_post, w_in=b_w_in[j].astype(BF16), b_in=b_b_in[j], v_g=b_v_norm_g[j],
                                   v_b=b_v_norm_b[j], w_s=b_w_s[j], b_s=b_b_s[j], w_out=b_w_out[j].astype(BF16),
                                   g=g1, b=b1, alpha=alpha)
            y_p, _ = cm(y_p, tm=tm_p, sample=False)
            y_s, v_rows = cm(y_s, tm=tm_s, sample=True)
            outs["v_s"].append(v_rows.reshape(db, t_new, -1))
        else:
            w_in, w_grp, w_out = c_w_in[j].astype(BF16), c_w_grp[j].astype(BF16), c_w_out[j].astype(BF16)
            y_p, last = _pool_prompt_post(y_p, w_in, w_grp, c_scale[j], w_out, g1, b1, alpha=alpha, tm=tm_p, seq=seq)
            nbuf = state_pool.shape[2]
            outs["pool_p"].append(last[:, last.shape[1] - nbuf:, :])
            y_s, h_new = _pool_sample_post(y_s, jnp.transpose(state_pool[j], (1, 0, 2)), w_in, w_grp, c_scale[j], w_out, g1, b1, alpha=alpha)
            outs["pool_s"].append(jnp.concatenate([state_pool[j][:, 1:], h_new[:, None, :]], axis=1))

        y_p = ffn(y_p, tm_p, i, 1, 2)
        y_s = ffn(y_s, tm_s, i, 1, 2)

    return (y_p.reshape(batch, seq, d), y_s.reshape(db, t_new, d),
            jnp.stack(outs["ckv_p"]), jnp.stack(outs["kr_p"]), jnp.stack(outs["ckv_s"]), jnp.stack(outs["kr_s"]),
            jnp.stack(outs["v_s"]), jnp.stack(outs["pool_p"]), jnp.stack(outs["pool_s"]))
```

```python
import functools

import jax
import jax.numpy as jnp
from jax import lax
from jax.experimental import pallas as pl
from jax.experimental.pallas import tpu as pltpu

F32 = jnp.float32
BF16 = jnp.bfloat16

LN_EPS = 1e-5
RMS_EPS = 1e-6
NEG_INF = -1e30
ROPE_THETA = 10000.0
POOL_WINDOWS = (2, 4, 8, 16)
CHUNK = 128

V7X_VMEM_BYTES = 64 * 1024 * 1024
VMEM_LIMIT = V7X_VMEM_BYTES - 8 * 1024 * 1024
LANES = 128
MXU_COLS = 256
ATTN_TQ = 256
ATTN_TK = 256


def _params(n_grid_dims):
    return pltpu.CompilerParams(
        dimension_semantics=("arbitrary",) * n_grid_dims, vmem_limit_bytes=VMEM_LIMIT)


def _const_spec(shape):
    nd = len(shape)
    return pl.BlockSpec(shape, lambda *_: (0,) * nd, pipeline_mode=pl.Buffered(1))


def _dot(a, b):
    return jnp.dot(a, b, preferred_element_type=F32)


def _dot_nt(a, b):
    return lax.dot_general(a, b, (((1,), (1,)), ((), ())), preferred_element_type=F32)


def _layernorm(x, g, b):
    mu = jnp.mean(x, axis=-1, keepdims=True)
    xc = x - mu
    var = jnp.mean(xc * xc, axis=-1, keepdims=True)
    return xc * lax.rsqrt(var + LN_EPS) * g + b


def _rmsnorm(x, g):
    return x * lax.rsqrt(jnp.mean(x * x, axis=-1, keepdims=True) + RMS_EPS) * g


def _ffn_kernel(x_ref, wg_ref, wu_ref, wd_ref, g_ref, b_ref, o_ref, *, alpha, bounds):
    x = x_ref[...]
    xb = x.astype(BF16)
    acc = None
    for lo, hi in zip(bounds[:-1], bounds[1:]):
        sl = slice(lo, hi)
        gate = _dot(xb, wg_ref[:, sl])
        up = _dot(xb, wu_ref[:, sl])
        h = (gate * jax.nn.sigmoid(gate) * up).astype(BF16)
        d = _dot(h, wd_ref[sl, :])
        acc = d if acc is None else acc + d
    o_ref[...] = _layernorm(alpha * x + 0.5 * acc, g_ref[...], b_ref[...])


def _ffn_chunk_bounds(d_ff, max_cols=1536):
    assert max_cols % MXU_COLS == 0
    bounds = list(range(0, d_ff, max_cols)) + [d_ff]
    return tuple(bounds)


def _ffn_post(x, wg, wu, wd, g, b, *, layer, half, alpha, tm):
    n, d = x.shape
    f = wg.shape[-1]
    row = pl.BlockSpec((tm, d), lambda i: (i, 0))

    def weight(k, m):
        return pl.BlockSpec((None, None, k, m), lambda i: (layer, half, 0, 0), pipeline_mode=pl.Buffered(1))

    return pl.pallas_call(
        functools.partial(_ffn_kernel, alpha=alpha, bounds=_ffn_chunk_bounds(f)),
        grid=(n // tm,),
        in_specs=[row, weight(d, f), weight(d, f), weight(f, d),
                  _const_spec((1, d)), _const_spec((1, d))],
        out_specs=row,
        out_shape=jax.ShapeDtypeStruct((n, d), F32),
        compiler_params=_params(1),
        name="ffn_post",
    )(x, wg, wu, wd, g, b)


def _mla_proj_kernel(x_ref, win_ref, qn_ref, kvn_ref, wuq_ref, wukt_ref, inv_ref, sgn_ref,
                     qlat_ref, qrope_ref, ckv_ref, kr_ref, *key_refs,
                     tm, tk, pos0, seq_len, q_rank, kv_rank, rope_dim, nope_dim, scale):
    heads = qlat_ref.shape[0]
    xb = x_ref[...].astype(BF16)
    lat = _dot(xb, win_ref[...])
    c_q = _rmsnorm(lat[:, :q_rank], qn_ref[...])
    c_kv = _rmsnorm(lat[:, q_rank:q_rank + kv_rank], kvn_ref[...])
    o_r = q_rank + kv_rank
    kr_raw = lat[:, o_r:o_r + rope_dim]
    kr_swp = lat[:, o_r + LANES:o_r + LANES + rope_dim]

    if seq_len == 1:
        pos = jnp.full((1, 1), pos0, F32)
    else:
        row = pl.program_id(0) * tm + lax.broadcasted_iota(jnp.int32, (tm, 1), 0)
        pos = (pos0 + row % seq_len).astype(F32)
    ang = pos * inv_ref[...]
    cos2 = jnp.cos(ang)
    sin2 = jnp.sin(ang) * sgn_ref[...]

    k_rope = kr_raw * cos2[:, :rope_dim] + kr_swp * sin2[:, :rope_dim]
    ckv_ref[...] = c_kv
    kr_ref[...] = k_rope
    if tk:
        ckvb_ref, krb_ref, vt_ref = key_refs
        ckvb_ref[...] = c_kv.astype(BF16)
        krb_ref[...] = k_rope.astype(BF16)
        for blk in range(tm // tk):
            vt_ref[blk] = c_kv[blk * tk:(blk + 1) * tk, :].T.astype(BF16)

    q = _dot(c_q.astype(BF16), wuq_ref[...])
    n_nope = heads * nope_dim
    n_rope = heads * rope_dim
    reps = n_rope // LANES
    cos_h = jnp.concatenate([cos2] * reps, axis=-1)
    sin_h = jnp.concatenate([sin2] * reps, axis=-1)
    q_rope = (q[:, n_nope:n_nope + n_rope] * cos_h + q[:, n_nope + n_rope:] * sin_h) * scale
    for h in range(heads):
        q_nope = q[:, h * nope_dim:(h + 1) * nope_dim].astype(BF16)
        qlat_ref[h] = (_dot(q_nope, wukt_ref[h]) * scale).astype(BF16)
        qrope_ref[h] = q_rope[:, h * rope_dim:(h + 1) * rope_dim].astype(BF16)


def _mla_weights(w_in, w_uq, w_uk, w_uv, w_o, *, q_rank, kv_rank, rope_dim, heads, nope_dim):
    half = rope_dim // 2
    d = w_in.shape[0]

    def swap(w):
        return jnp.concatenate([w[..., half:], w[..., :half]], axis=-1)

    w_r = w_in[:, q_rank + kv_rank:]
    zpad = jnp.zeros((d, LANES - rope_dim), w_in.dtype)
    w_in_x = jnp.concatenate([w_in[:, :q_rank + kv_rank], w_r, zpad, swap(w_r), zpad], axis=1)
    uq = w_uq.reshape(q_rank, heads, nope_dim + rope_dim)
    uq_n = uq[:, :, :nope_dim].reshape(q_rank, heads * nope_dim)
    uq_r = uq[:, :, nope_dim:]
    w_uq_x = jnp.concatenate([uq_n, uq_r.reshape(q_rank, heads * rope_dim),
                              swap(uq_r).reshape(q_rank, heads * rope_dim)], axis=1)
    w_uk_t = jnp.transpose(w_uk, (1, 2, 0))
    w_uv_h = jnp.transpose(w_uv, (1, 0, 2))
    return (w_in_x.astype(BF16), w_uq_x.astype(BF16), w_uk_t.astype(BF16),
            w_uv_h.astype(BF16), w_o.astype(BF16))


def _rope_tables(rope_dim):
    half = rope_dim // 2
    inv = ROPE_THETA ** (-jnp.arange(half, dtype=F32) / half)
    reps = LANES // half
    inv_l = jnp.tile(inv, reps)[None, :]
    sgn = jnp.tile(jnp.concatenate([-jnp.ones((half,), F32), jnp.ones((half,), F32)]), reps // 2)[None, :]
    return inv_l, sgn


def _mla_project(x, w_in_x, q_norm, kv_norm, w_uq_x, w_uk_t, *, tm, tk, pos0, seq_len, rope_dim, scale):
    n, d = x.shape
    heads, nope_dim, kv_rank = w_uk_t.shape
    q_rank = w_uq_x.shape[0]
    inv_l, sgn = _rope_tables(rope_dim)
    row = lambda w: pl.BlockSpec((tm, w), lambda i: (i, 0))
    hrow = lambda w: pl.BlockSpec((heads, tm, w), lambda i: (0, i, 0))
    kern = functools.partial(
        _mla_proj_kernel, tm=tm, tk=tk, pos0=pos0, seq_len=seq_len, q_rank=q_rank, kv_rank=kv_rank,
        rope_dim=rope_dim, nope_dim=nope_dim, scale=scale)
    out_specs = [hrow(kv_rank), hrow(rope_dim), row(kv_rank), row(rope_dim)]
    out_shape = [jax.ShapeDtypeStruct((heads, n, kv_rank), BF16),
                 jax.ShapeDtypeStruct((heads, n, rope_dim), BF16),
                 jax.ShapeDtypeStruct((n, kv_rank), F32),
                 jax.ShapeDtypeStruct((n, rope_dim), F32)]
    if tk:
        assert tm % tk == 0
        out_specs += [row(kv_rank), row(rope_dim), pl.BlockSpec((tm // tk, kv_rank, tk), lambda i: (i, 0, 0))]
        out_shape += [jax.ShapeDtypeStruct((n, kv_rank), BF16), jax.ShapeDtypeStruct((n, rope_dim), BF16),
                      jax.ShapeDtypeStruct((n // tk, kv_rank, tk), BF16)]
    return pl.pallas_call(
        kern,
        grid=(n // tm,),
        in_specs=[row(d), _const_spec(w_in_x.shape), _const_spec((1, q_rank)), _const_spec((1, kv_rank)),
                  _const_spec(w_uq_x.shape), _const_spec(w_uk_t.shape),
                  _const_spec((1, LANES)), _const_spec((1, LANES))],
        out_specs=out_specs,
        out_shape=out_shape,
        compiler_params=_params(1),
        name="mla_project",
    )(x, w_in_x, q_norm[None, :], kv_norm[None, :], w_uq_x, w_uk_t, inv_l, sgn)


def _prompt_attn_kernel(ql_ref, qr_ref, kc_ref, kr_ref, vt_ref, o_ref, m_sc, l_sc, acc_sc, *, tq, tk):
    heads, _, c = ql_ref.shape
    qi = pl.program_id(1)
    ql = ql_ref[...].reshape(heads * tq, c)
    qr = qr_ref[...].reshape(heads * tq, qr_ref.shape[2])
    m_sc[...] = jnp.full(m_sc.shape, NEG_INF, F32)
    l_sc[...] = jnp.zeros(l_sc.shape, F32)
    acc_sc[...] = jnp.zeros(acc_sc.shape, F32)

    def step(j, masked):
        k0 = pl.multiple_of(j * tk, tk)
        s = _dot_nt(kc_ref[pl.ds(k0, tk), :], ql) + _dot_nt(kr_ref[pl.ds(k0, tk), :], qr)
        if masked:
            k_pos = k0 + lax.broadcasted_iota(jnp.int32, s.shape, 0)
            q_pos = qi * tq + (lax.broadcasted_iota(jnp.int32, s.shape, 1) & (tq - 1))
            s = jnp.where(k_pos <= q_pos, s, NEG_INF)
        m_prev = m_sc[...]
        m_new = jnp.maximum(m_prev, jnp.max(s, axis=0, keepdims=True))
        a = jnp.exp(m_prev - m_new)
        p = jnp.exp(s - m_new)
        l_sc[...] = a * l_sc[...] + jnp.sum(p, axis=0, keepdims=True)
        acc_sc[...] = a * acc_sc[...] + _dot(vt_ref[j], p.astype(BF16))
        m_sc[...] = m_new

    n_full = (qi * tq) // tk

    def body(j, carry):
        step(j, False)
        return carry

    lax.fori_loop(0, n_full, body, 0)
    step(n_full, True)

    out = (acc_sc[...] / l_sc[...]).T
    for h in range(heads):
        o_ref[:, h * c:(h + 1) * c] = out[h * tq:(h + 1) * tq, :].astype(o_ref.dtype)


def _prompt_attention(q_lat, q_rope, kc, kr, vt, *, batch, seq, tq, tk):
    heads, n, c = q_lat.shape
    r = q_rope.shape[2]
    assert tq & (tq - 1) == 0 and tk % tq == 0 and seq % tk == 0 and vt.shape == (n // tk, c, tk)
    nq = seq // tq
    nk = seq // tk
    return pl.pallas_call(
        functools.partial(_prompt_attn_kernel, tq=tq, tk=tk),
        grid=(batch, nq),
        in_specs=[pl.BlockSpec((heads, tq, c), lambda b, i: (0, b * nq + i, 0)),
                  pl.BlockSpec((heads, tq, r), lambda b, i: (0, b * nq + i, 0)),
                  pl.BlockSpec((seq, c), lambda b, i: (b, 0)),
                  pl.BlockSpec((seq, r), lambda b, i: (b, 0)),
                  pl.BlockSpec((nk, c, tk), lambda b, i: (b, 0, 0))],
        out_specs=pl.BlockSpec((tq, heads * c), lambda b, i: (b * nq + i, 0)),
        out_shape=jax.ShapeDtypeStruct((n, heads * c), BF16),
        scratch_shapes=[pltpu.VMEM((1, heads * tq), F32), pltpu.VMEM((1, heads * tq), F32),
                        pltpu.VMEM((c, heads * tq), F32)],
        compiler_params=_params(2),
        name="prompt_attention",
    )(q_lat, q_rope, kc, kr, vt)


def _sample_attn_kernel(pt_ref, ql_ref, qr_ref, cnew_ref, rnew_ref, ckv_hbm, krt_hbm, o_ref,
                        kbuf, rbuf, sem, *, layer, n_pages, page, n_chunks):
    b = pl.program_id(0)
    nb = pl.num_programs(0)
    slot = b % 2

    def page_copies(row, dst_slot, p):
        pg = pt_ref[row * n_pages + p]
        keys = pl.ds(p * page, page)
        return (pltpu.make_async_copy(ckv_hbm.at[layer, pg], kbuf.at[dst_slot, keys], sem.at[dst_slot, 0]),
                pltpu.make_async_copy(krt_hbm.at[layer, pg], rbuf.at[dst_slot, :, keys], sem.at[dst_slot, 1]))

    def start_row(row, dst_slot):
        for p in range(n_pages):
            for cp in page_copies(row, dst_slot, p):
                cp.start()

    @pl.when(b == 0)
    def _():
        start_row(0, 0)

    @pl.when(b + 1 < nb)
    def _():
        start_row(b + 1, 1 - slot)

    for p in range(n_pages):
        for cp in page_copies(b, slot, p):
            cp.wait()

    ql = ql_ref[0]
    qr = qr_ref[0]
    c_new = cnew_ref[0]
    r_new = rnew_ref[0]
    m = (jnp.sum(ql.astype(F32) * c_new, axis=-1, keepdims=True)
         + jnp.sum(qr.astype(F32) * r_new, axis=-1, keepdims=True))
    l = jnp.ones_like(m)
    acc = jnp.broadcast_to(c_new, (ql.shape[0], c_new.shape[1]))
    ck = (n_pages * page) // n_chunks
    for c in range(n_chunks):
        keys = pl.ds(c * ck, ck)
        kc = kbuf[slot, keys, :].astype(BF16)
        s = _dot_nt(ql, kc) + _dot(qr, rbuf[slot, :, keys].astype(BF16))
        m_new = jnp.maximum(m, jnp.max(s, axis=-1, keepdims=True))
        a = jnp.exp(m - m_new)
        p = jnp.exp(s - m_new)
        l = a * l + jnp.sum(p, axis=-1, keepdims=True)
        acc = a * acc + _dot(p.astype(BF16), kc)
        m = m_new
    o_ref[0] = (acc / l).astype(o_ref.dtype)


def _sample_attention(page_table, q_lat, q_rope, c_new, r_new, cache_kv, cache_kr_t, *, layer):
    db, heads, c = q_lat.shape
    r = q_rope.shape[2]
    n_pages = page_table.shape[1]
    page = cache_kv.shape[2]
    assert cache_kr_t.shape[2:] == (r, page)
    past = n_pages * page
    n_chunks = max(1, past // 1024)
    blk = lambda h, w: pl.BlockSpec((1, h, w), lambda b, pt: (b, 0, 0))
    grid_spec = pltpu.PrefetchScalarGridSpec(
        num_scalar_prefetch=1,
        grid=(db,),
        in_specs=[blk(heads, c), blk(heads, r), blk(1, c), blk(1, r),
                  pl.BlockSpec(memory_space=pl.ANY), pl.BlockSpec(memory_space=pl.ANY)],
        out_specs=blk(heads, c),
        scratch_shapes=[pltpu.VMEM((2, past, c), F32), pltpu.VMEM((2, r, past), F32),
                        pltpu.SemaphoreType.DMA((2, 2))],
    )
    return pl.pallas_call(
        functools.partial(_sample_attn_kernel, layer=layer, n_pages=n_pages, page=page, n_chunks=n_chunks),
        grid_spec=grid_spec,
        out_shape=jax.ShapeDtypeStruct((db, heads, c), BF16),
        compiler_params=_params(1),
        name="sample_attention",
    )(page_table.reshape(-1), q_lat, q_rope, c_new, r_new, cache_kv, cache_kr_t)


def _mla_out_kernel(o_ref, x_ref, wuv_ref, wo_ref, g_ref, b_ref, y_ref, *, alpha):
    heads, c, _ = wuv_ref.shape
    parts = [_dot(o_ref[:, h * c:(h + 1) * c], wuv_ref[h]).astype(BF16) for h in range(heads)]
    mix = _dot(jnp.concatenate(parts, axis=-1), wo_ref[...])
    y_ref[...] = _layernorm(alpha * x_ref[...] + mix, g_ref[...], b_ref[...])


def _mla_out_post(o_lat, x, w_uv_h, w_o, g, b, *, alpha, tm):
    n, d = x.shape
    row = lambda w: pl.BlockSpec((tm, w), lambda i: (i, 0))
    return pl.pallas_call(
        functools.partial(_mla_out_kernel, alpha=alpha),
        grid=(n // tm,),
        in_specs=[row(o_lat.shape[1]), row(d), _const_spec(w_uv_h.shape), _const_spec(w_o.shape),
                  _const_spec((1, d)), _const_spec((1, d))],
        out_specs=row(d),
        out_shape=jax.ShapeDtypeStruct((n, d), F32),
        compiler_params=_params(1),
        name="mla_out_post",
    )(o_lat, x, w_uv_h, w_o, g, b)


def _cmlp_kernel(*refs, alpha, tm, heads, sample):
    if sample:
        (x_ref, win_ref, bin_ref, vg_ref, vb_ref, wcol_ref, bcol_ref, wout_ref, g_ref, b_ref,
         y_ref, v_ref) = refs
    else:
        (x_ref, win_ref, bin_ref, vg_ref, vb_ref, ws_ref, bs_ref, wout_ref, g_ref, b_ref,
         y_ref, vn_sc) = refs
    dc = wout_ref.shape[0]
    hd = dc // heads
    x = x_ref[...]
    xb = x.astype(BF16)
    v = jax.nn.gelu(_dot(xb, win_ref[:, dc:]) + bin_ref[:, dc:])
    vn = _layernorm(v, vg_ref[...], vb_ref[...])
    if sample:
        v_ref[...] = vn
        mixed = vn * wcol_ref[...] + bcol_ref[...]
        u = jax.nn.gelu(_dot(xb, win_ref[:, :dc]) + bin_ref[:, :dc])
        mix = _dot((u * mixed).astype(BF16), wout_ref[...])
    else:
        vn_sc[...] = vn.astype(BF16)
        ri = lax.broadcasted_iota(jnp.int32, (CHUNK, CHUNK), 0)
        ci = lax.broadcasted_iota(jnp.int32, (CHUNK, CHUNK), 1)
        mix = None
        group = next(k for k in range(1, heads + 1) if heads % k == 0 and (k * hd) % MXU_COLS == 0)
        for h0 in range(0, heads, group):
            parts = []
            for h in range(h0, h0 + group):
                cols = slice(h * hd, (h + 1) * hd)
                w_tril = jnp.where(ci <= ri, ws_ref[h], 0.0).astype(BF16)
                bias = bs_ref[h]
                parts.append(jnp.concatenate(
                    [_dot(w_tril, vn_sc[c * CHUNK:(c + 1) * CHUNK, cols]) + bias for c in range(tm // CHUNK)],
                    axis=0))
            mixed = jnp.concatenate(parts, axis=-1)
            gcols = slice(h0 * hd, (h0 + group) * hd)
            u = jax.nn.gelu(_dot(xb, win_ref[:, gcols]) + bin_ref[:, gcols])
            d = _dot((u * mixed).astype(BF16), wout_ref[gcols, :])
            mix = d if mix is None else mix + d
    y_ref[...] = _layernorm(alpha * x + mix, g_ref[...], b_ref[...])


def _chunk_mlp_post(x, w_in, b_in, v_g, v_b, w_s, b_s, w_out, g, b, *, alpha, tm, sample):
    n, d = x.shape
    dc = w_out.shape[0]
    heads = w_s.shape[0]
    row = pl.BlockSpec((tm, d), lambda i: (i, 0))
    common = [row, _const_spec(w_in.shape), _const_spec((1, 2 * dc)), _const_spec((1, dc)), _const_spec((1, dc))]
    tail = [_const_spec(w_out.shape), _const_spec((1, d)), _const_spec((1, d))]
    kern = functools.partial(_cmlp_kernel, alpha=alpha, tm=tm, heads=heads, sample=sample)
    y_shape = jax.ShapeDtypeStruct((n, d), F32)
    if sample:
        hd = dc // heads
        wcol = jnp.repeat(w_s[:, 0, 0], hd)[None, :]
        bcol = jnp.repeat(b_s[:, 0], hd)[None, :]
        return pl.pallas_call(
            kern, grid=(n // tm,),
            in_specs=common + [_const_spec((1, dc)), _const_spec((1, dc))] + tail,
            out_specs=[row, pl.BlockSpec((tm, dc), lambda i: (i, 0))],
            out_shape=[y_shape, jax.ShapeDtypeStruct((n, dc), F32)],
            compiler_params=_params(1), name="chunk_mlp_sample",
        )(x, w_in, b_in[None, :], v_g[None, :], v_b[None, :], wcol, bcol, w_out, g, b)
    assert tm % CHUNK == 0
    y = pl.pallas_call(
        kern, grid=(n // tm,),
        in_specs=common + [_const_spec(w_s.shape), _const_spec((heads, CHUNK, 1))] + tail,
        out_specs=row, out_shape=y_shape,
        scratch_shapes=[pltpu.VMEM((tm, dc), BF16)],
        compiler_params=_params(1), name="chunk_mlp_prompt",
    )(x, w_in, b_in[None, :], v_g[None, :], v_b[None, :], w_s, b_s[:, :, None], w_out, g, b)
    return y, None


def _pool_tail(pooled_groups, x, wgrp_ref, scale_ref, wout_ref, g_ref, b_ref, alpha):
    z = jnp.concatenate(
        [_dot(p.astype(BF16), wgrp_ref[gi]) for gi, p in enumerate(pooled_groups)], axis=-1)
    mix = _dot((z * scale_ref[...]).astype(BF16), wout_ref[...])
    return _layernorm(alpha * x + mix, g_ref[...], b_ref[...])


def _pool_prompt_kernel(x_ref, halo_ref, win_ref, wgrp_ref, scale_ref, wout_ref, g_ref, b_ref,
                        y_ref, last_ref, h_sc, *, alpha, tm, tiles_per_seq, halo):
    t0 = (pl.program_id(0) % tiles_per_seq) * tm
    x = x_ref[...]
    h = _dot(x.astype(BF16), win_ref[...])
    h_halo = _dot(halo_ref[...].astype(BF16), win_ref[...])
    h_sc[:halo, :] = jnp.where(t0 > 0, h_halo, 0.0)
    h_sc[halo:, :] = h
    last_ref[0] = h[tm - halo:, :]
    gd = h.shape[1] // len(POOL_WINDOWS)
    t = t0 + lax.broadcasted_iota(jnp.int32, (tm, 1), 0)
    pooled = []
    for gi, w in enumerate(POOL_WINDOWS):
        cols = slice(gi * gd, (gi + 1) * gd)
        tot = h[:, cols]
        for k in range(1, w):
            tot = tot + h_sc[halo - k:halo - k + tm, cols]
        cnt = jnp.minimum(t + 1, w).astype(F32)
        pooled.append(tot / cnt - h[:, cols])
    y_ref[...] = _pool_tail(pooled, x, wgrp_ref, scale_ref, wout_ref, g_ref, b_ref, alpha)


def _pool_prompt_post(x, w_in, w_grp, scale, w_out, g, b, *, alpha, tm, seq):
    n, d = x.shape
    dp = w_in.shape[1]
    halo = max(POOL_WINDOWS)
    assert seq % tm == 0 and tm % halo == 0 and halo % 8 == 0
    tiles_per_seq = seq // tm
    hb = tm // halo
    row = pl.BlockSpec((tm, d), lambda i: (i, 0))
    kern = functools.partial(_pool_prompt_kernel, alpha=alpha, tm=tm, tiles_per_seq=tiles_per_seq, halo=halo)
    return pl.pallas_call(
        kern, grid=(n // tm,),
        in_specs=[row, pl.BlockSpec((halo, d), lambda i: (jnp.maximum(i * hb - 1, 0), 0)),
                  _const_spec(w_in.shape), _const_spec(w_grp.shape), _const_spec((1, dp)),
                  _const_spec(w_out.shape), _const_spec((1, d)), _const_spec((1, d))],
        out_specs=[row, pl.BlockSpec((1, halo, dp), lambda i: (i // tiles_per_seq, 0, 0))],
        out_shape=[jax.ShapeDtypeStruct((n, d), F32),
                   jax.ShapeDtypeStruct((n // seq, halo, dp), F32)],
        scratch_shapes=[pltpu.VMEM((tm + halo, dp), F32)],
        compiler_params=_params(1), name="pool_prompt",
    )(x, x, w_in, w_grp, scale[None, :], w_out, g, b)


def _pool_sample_kernel(x_ref, st_ref, win_ref, wgrp_ref, scale_ref, wout_ref, g_ref, b_ref,
                        y_ref, h_ref, *, alpha):
    x = x_ref[...]
    h = _dot(x.astype(BF16), win_ref[...])
    h_ref[...] = h
    nbuf = st_ref.shape[0]
    gd = h.shape[1] // len(POOL_WINDOWS)
    pooled = []
    for gi, w in enumerate(POOL_WINDOWS):
        cols = slice(gi * gd, (gi + 1) * gd)
        tot = h[:, cols]
        for k in range(1, w):
            tot = tot + st_ref[nbuf - k, :, cols]
        pooled.append(tot / float(w) - h[:, cols])
    y_ref[...] = _pool_tail(pooled, x, wgrp_ref, scale_ref, wout_ref, g_ref, b_ref, alpha)


def _pool_sample_post(x, state, w_in, w_grp, scale, w_out, g, b, *, alpha):
    n, d = x.shape
    dp = w_in.shape[1]
    assert state.shape[0] == max(POOL_WINDOWS) - 1
    return pl.pallas_call(
        functools.partial(_pool_sample_kernel, alpha=alpha), grid=(1,),
        in_specs=[_const_spec((n, d)), _const_spec(state.shape), _const_spec(w_in.shape),
                  _const_spec(w_grp.shape), _const_spec((1, dp)), _const_spec(w_out.shape),
                  _const_spec((1, d)), _const_spec((1, d))],
        out_specs=[pl.BlockSpec((n, d), lambda i: (0, 0)), pl.BlockSpec((n, dp), lambda i: (0, 0))],
        out_shape=[jax.ShapeDtypeStruct((n, d), F32), jax.ShapeDtypeStruct((n, dp), F32)],
        compiler_params=_params(1), name="pool_sample",
    )(x, state, w_in, w_grp, scale[None, :], w_out, g, b)


def _row_tile(n, want):
    tm = min(n, want)
    assert n % tm == 0
    return tm


def kernel(x_prompt, x_sample, cache_kv_latent, cache_k_rope, state_pool, page_table, ln_g, ln_b, ffn_w_gate, ffn_w_up, ffn_w_down, a_w_in, a_q_norm, a_kv_norm, a_w_uq, a_w_uk, a_w_uv, a_w_o, b_w_in, b_b_in, b_v_norm_g, b_v_norm_b, b_w_s, b_b_s, b_w_out, c_w_in, c_w_grp, c_scale, c_w_out):
    batch, seq, d = x_prompt.shape
    db, t_new, _ = x_sample.shape
    assert t_new == 1, "decode attention handles one new token per sample row"
    depth = ln_g.shape[0]
    alpha = (2 * depth) ** 0.25
    kv_rank, heads, nope_dim = a_w_uk.shape[1:]
    rope_dim = cache_k_rope.shape[-1]
    q_rank = a_w_uq.shape[1]
    scale = (nope_dim + rope_dim) ** -0.5
    past_len = page_table.shape[1] * cache_kv_latent.shape[2]
    cmlp_heads = b_w_s.shape[1]
    n_p, n_s = batch * seq, db * t_new
    tm_p, tm_s = _row_tile(n_p, 512), _row_tile(n_s, 512)

    y_p = x_prompt.reshape(n_p, d)
    y_s = x_sample.reshape(n_s, d)
    wg, wu, wd = ffn_w_gate.astype(BF16), ffn_w_up.astype(BF16), ffn_w_down.astype(BF16)
    cache_kr_t = jnp.swapaxes(cache_k_rope, 2, 3)
    outs = {k: [] for k in ("ckv_p", "kr_p", "ckv_s", "kr_s", "v_s", "pool_p", "pool_s")}

    def ffn(y, tm, i, k, ln_idx):
        return _ffn_post(y, wg, wu, wd, ln_g[i, ln_idx][None, :], ln_b[i, ln_idx][None, :],
                         layer=i, half=k, alpha=alpha, tm=tm)

    for i in range(depth):
        kind, j = i % 3, i // 3
        y_p = ffn(y_p, tm_p, i, 0, 0)
        y_s = ffn(y_s, tm_s, i, 0, 0)
        g1, b1 = ln_g[i, 1][None, :], ln_b[i, 1][None, :]

        if kind == 0:
            w_in_x, w_uq_x, w_uk_t, w_uv_h, w_o = _mla_weights(
                a_w_in[j], a_w_uq[j], a_w_uk[j], a_w_uv[j], a_w_o[j],
                q_rank=q_rank, kv_rank=kv_rank, rope_dim=rope_dim, heads=heads, nope_dim=nope_dim)
            proj = functools.partial(_mla_project, w_in_x=w_in_x, q_norm=a_q_norm[j], kv_norm=a_kv_norm[j],
                                     w_uq_x=w_uq_x, w_uk_t=w_uk_t, rope_dim=rope_dim, scale=scale)
            ql, qr, ckv, kr, ckv_b, kr_b, vt = proj(y_p, tm=tm_p, tk=ATTN_TK, pos0=0, seq_len=seq)
            o_lat = _prompt_attention(ql, qr, ckv_b, kr_b, vt, batch=batch, seq=seq, tq=ATTN_TQ, tk=ATTN_TK)
            y_p = _mla_out_post(o_lat, y_p, w_uv_h, w_o, g1, b1, alpha=alpha, tm=tm_p)
            outs["ckv_p"].append(ckv.reshape(batch, seq, kv_rank))
            outs["kr_p"].append(kr.reshape(batch, seq, rope_dim))

            ql, qr, ckv, kr = proj(y_s, tm=tm_s, tk=0, pos0=past_len, seq_len=t_new)
            o_s = _sample_attention(page_table, jnp.transpose(ql, (1, 0, 2)), jnp.transpose(qr, (1, 0, 2)),
                                    ckv[:, None, :], kr[:, None, :], cache_kv_latent, cache_kr_t, layer=j)
            y_s = _mla_out_post(o_s.reshape(n_s, heads * kv_rank), y_s, w_uv_h, w_o, g1, b1, alpha=alpha, tm=tm_s)
            outs["ckv_s"].append(ckv.reshape(db, t_new, kv_rank))
            outs["kr_s"].append(kr.reshape(db, t_new, rope_dim))
        elif kind == 1:
            cm = functools.partial(_chunk_mlp_post, w_in=b_w_in[j].astype(BF16), b_in=b_b_in[j], v_g=b_v_norm_g[j],
                                   v_b=b_v_norm_b[j], w_s=b_w_s[j], b_s=b_b_s[j], w_out=b_w_out[j].astype(BF16),
                                   g=g1, b=b1, alpha=alpha)
            y_p, _ = cm(y_p, tm=tm_p, sample=False)
            y_s, v_rows = cm(y_s, tm=tm_s, sample=True)
            outs["v_s"].append(v_rows.reshape(db, t_new, -1))
        else:
            w_in, w_grp, w_out = c_w_in[j].astype(BF16), c_w_grp[j].astype(BF16), c_w_out[j].astype(BF16)
            y_p, last = _pool_prompt_post(y_p, w_in, w_grp, c_scale[j], w_out, g1, b1, alpha=alpha, tm=tm_p, seq=seq)
            nbuf = state_pool.shape[2]
            outs["pool_p"].append(last[:, last.shape[1] - nbuf:, :])
            y_s, h_new = _pool_sample_post(y_s, jnp.transpose(state_pool[j], (1, 0, 2)), w_in, w_grp, c_scale[j], w_out, g1, b1, alpha=alpha)
            outs["pool_s"].append(jnp.concatenate([state_pool[j][:, 1:], h_new[:, None, :]], axis=1))

        y_p = ffn(y_p, tm_p, i, 1, 2)
        y_s = ffn(y_s, tm_s, i, 1, 2)

    return (y_p.reshape(batch, seq, d), y_s.reshape(db, t_new, d),
            jnp.stack(outs["ckv_p"]), jnp.stack(outs["kr_p"]), jnp.stack(outs["ckv_s"]), jnp.stack(outs["kr_s"]),
            jnp.stack(outs["v_s"]), jnp.stack(outs["pool_p"]), jnp.stack(outs["pool_s"]))
```

```python
import functools
import math

import jax
import jax.numpy as jnp
from jax import lax
from jax.experimental import pallas as pl
from jax.experimental.pallas import tpu as pltpu

F32 = jnp.float32
BF16 = jnp.bfloat16

LN_EPS = 1e-5
RMS_EPS = 1e-6
NEG_INF = -1e30
ROPE_THETA = 10000.0
POOL_WINDOWS = (2, 4, 8, 16)
CHUNK = 128

V7X_VMEM_BYTES = 64 * 1024 * 1024
VMEM_LIMIT = V7X_VMEM_BYTES - 8 * 1024 * 1024
LANES = 128
MXU_COLS = 256
ATTN_TQ = 256
ATTN_TK = 256
DECODE_CHUNK = 8192
FFN_TILE = 1024
FFN_SUB = 256


def _params(n_grid_dims):
    return pltpu.CompilerParams(
        dimension_semantics=("arbitrary",) * n_grid_dims, vmem_limit_bytes=VMEM_LIMIT)


def _const_spec(shape):
    nd = len(shape)
    return pl.BlockSpec(shape, lambda *_: (0,) * nd, pipeline_mode=pl.Buffered(1))


def _dot(a, b):
    return jnp.dot(a, b, preferred_element_type=F32)


def _dot_nt(a, b):
    return lax.dot_general(a, b, (((1,), (1,)), ((), ())), preferred_element_type=F32)


def _layernorm(x, g, b):
    mu = jnp.mean(x, axis=-1, keepdims=True)
    xc = x - mu
    var = jnp.mean(xc * xc, axis=-1, keepdims=True)
    return xc * lax.rsqrt(var + LN_EPS) * g + b


def _rmsnorm(x, g):
    return x * lax.rsqrt(jnp.mean(x * x, axis=-1, keepdims=True) + RMS_EPS) * g


def _ffn_kernel(x_ref, wgu_ref, wd_ref, g_ref, b_ref, o_ref, *, alpha, bounds, sub):
    for r in range(x_ref.shape[0] // sub):
        rows = slice(r * sub, (r + 1) * sub)
        x = x_ref[rows, :]
        xb = x.astype(BF16)
        acc = None
        for lo, hi in zip(bounds[:-1], bounds[1:]):
            gu = _dot(xb, wgu_ref[:, 2 * lo:2 * hi])
            gate, up = gu[:, :hi - lo], gu[:, hi - lo:]
            h = (gate * jax.nn.sigmoid(gate) * up).astype(BF16)
            d = _dot(h, wd_ref[lo:hi, :])
            acc = d if acc is None else acc + d
        o_ref[rows, :] = _layernorm(alpha * x + 0.5 * acc, g_ref[...], b_ref[...])


def _ffn_chunk_bounds(d_ff, max_cols=1536):
    bounds = tuple(range(0, d_ff, max_cols)) + (d_ff,)
    assert all((2 * (hi - lo)) % MXU_COLS == 0 for lo, hi in zip(bounds[:-1], bounds[1:]))
    return bounds


def _ffn_gate_up(wg, wu):
    bounds = _ffn_chunk_bounds(wg.shape[-1])
    parts = []
    for lo, hi in zip(bounds[:-1], bounds[1:]):
        parts += [wg[..., lo:hi], wu[..., lo:hi]]
    return jnp.concatenate(parts, axis=-1).astype(BF16)


def _ffn_post(x, wgu, wd, g, b, *, layer, half, alpha, tm, sub):
    n, d = x.shape
    f = wd.shape[-2]
    assert tm % sub == 0
    row = pl.BlockSpec((tm, d), lambda i: (i, 0))

    def weight(k, m):
        return pl.BlockSpec((None, None, k, m), lambda i: (layer, half, 0, 0), pipeline_mode=pl.Buffered(1))

    return pl.pallas_call(
        functools.partial(_ffn_kernel, alpha=alpha, bounds=_ffn_chunk_bounds(f), sub=sub),
        grid=(n // tm,),
        in_specs=[row, weight(d, 2 * f), weight(f, d), _const_spec((1, d)), _const_spec((1, d))],
        out_specs=row,
        out_shape=jax.ShapeDtypeStruct((n, d), F32),
        compiler_params=_params(1),
        name="ffn_post",
    )(x, wgu, wd, g, b)


def _mla_proj_kernel(x_ref, win_ref, qn_ref, kvn_ref, wuq_ref, wukt_ref, inv_ref, sgn_ref,
                     qcat_ref, ckv_ref, kr_ref, *rest,
                     tm, tk, pos0, seq_len, q_rank, kv_rank, rope_dim, nope_dim, scale):
    key_refs, rope_sc = (rest[:2], rest[2:]) if tk else ((), rest)
    heads = qcat_ref.shape[0]
    xb = x_ref[...].astype(BF16)
    lat = _dot(xb, win_ref[...])
    c_q = _rmsnorm(lat[:, :q_rank], qn_ref[...])
    c_kv = _rmsnorm(lat[:, q_rank:q_rank + kv_rank], kvn_ref[...])
    o_r = q_rank + kv_rank
    kr_raw = lat[:, o_r:o_r + rope_dim]
    kr_swp = lat[:, o_r + LANES:o_r + LANES + rope_dim]

    if seq_len == 1:
        ang = jnp.full((1, 1), pos0, F32) * inv_ref[...]
        cos2 = jnp.cos(ang)
        sin2 = jnp.sin(ang) * sgn_ref[...]
    else:
        cos_off, sin_off = rope_sc

        @pl.when(pl.program_id(0) == 0)
        def _():
            off = lax.broadcasted_iota(jnp.int32, (tm, 1), 0).astype(F32) * inv_ref[...]
            cos_off[...] = jnp.cos(off)
            sin_off[...] = jnp.sin(off)

        start = pos0 + (pl.program_id(0) * tm) % seq_len
        ang0 = jnp.full((1, 1), start, jnp.int32).astype(F32) * inv_ref[...]
        cos0, sin0 = jnp.cos(ang0), jnp.sin(ang0)
        cos2 = cos0 * cos_off[...] - sin0 * sin_off[...]
        sin2 = (sin0 * cos_off[...] + cos0 * sin_off[...]) * sgn_ref[...]

    k_rope = kr_raw * cos2[:, :rope_dim] + kr_swp * sin2[:, :rope_dim]
    ckv_ref[...] = c_kv
    kr_ref[...] = k_rope
    if tk:
        kcat_ref, vt_ref = key_refs
        kcat_ref[:, :kv_rank] = c_kv.astype(BF16)
        kcat_ref[:, kv_rank:] = k_rope.astype(BF16)
        for blk in range(tm // tk):
            vt_ref[blk] = c_kv[blk * tk:(blk + 1) * tk, :].T.astype(BF16)

    q = _dot(c_q.astype(BF16), wuq_ref[...])
    n_nope = heads * nope_dim
    n_rope = heads * rope_dim
    reps = n_rope // LANES
    cos_h = jnp.concatenate([cos2] * reps, axis=-1)
    sin_h = jnp.concatenate([sin2] * reps, axis=-1)
    q_rope = (q[:, n_nope:n_nope + n_rope] * cos_h + q[:, n_nope + n_rope:] * sin_h) * scale
    for h in range(heads):
        q_nope = q[:, h * nope_dim:(h + 1) * nope_dim].astype(BF16)
        qcat_ref[h, :, :kv_rank] = (_dot(q_nope, wukt_ref[h]) * scale).astype(BF16)
        qcat_ref[h, :, kv_rank:] = q_rope[:, h * rope_dim:(h + 1) * rope_dim].astype(BF16)


def _mla_weights(w_in, w_uq, w_uk, w_uv, w_o, *, q_rank, kv_rank, rope_dim, heads, nope_dim):
    half = rope_dim // 2
    d = w_in.shape[0]

    def swap(w):
        return jnp.concatenate([w[..., half:], w[..., :half]], axis=-1)

    w_r = w_in[:, q_rank + kv_rank:]
    zpad = jnp.zeros((d, LANES - rope_dim), w_in.dtype)
    w_in_x = jnp.concatenate([w_in[:, :q_rank + kv_rank], w_r, zpad, swap(w_r), zpad], axis=1)
    uq = w_uq.reshape(q_rank, heads, nope_dim + rope_dim)
    uq_n = uq[:, :, :nope_dim].reshape(q_rank, heads * nope_dim)
    uq_r = uq[:, :, nope_dim:]
    w_uq_x = jnp.concatenate([uq_n, uq_r.reshape(q_rank, heads * rope_dim),
                              swap(uq_r).reshape(q_rank, heads * rope_dim)], axis=1)
    w_uk_t = jnp.transpose(w_uk, (1, 2, 0))
    w_uv_h = jnp.transpose(w_uv, (1, 0, 2))
    return (w_in_x.astype(BF16), w_uq_x.astype(BF16), w_uk_t.astype(BF16),
            w_uv_h.astype(BF16), w_o.astype(BF16))


def _rope_tables(rope_dim):
    half = rope_dim // 2
    inv = ROPE_THETA ** (-jnp.arange(half, dtype=F32) / half)
    reps = LANES // half
    inv_l = jnp.tile(inv, reps)[None, :]
    sgn = jnp.tile(jnp.concatenate([-jnp.ones((half,), F32), jnp.ones((half,), F32)]), reps // 2)[None, :]
    return inv_l, sgn


def _mla_project(x, w_in_x, q_norm, kv_norm, w_uq_x, w_uk_t, *, tm, tk, pos0, seq_len, rope_dim, scale):
    n, d = x.shape
    heads, nope_dim, kv_rank = w_uk_t.shape
    q_rank = w_uq_x.shape[0]
    inv_l, sgn = _rope_tables(rope_dim)
    row = lambda w: pl.BlockSpec((tm, w), lambda i: (i, 0))
    hrow = lambda w: pl.BlockSpec((heads, tm, w), lambda i: (0, i, 0))
    kern = functools.partial(
        _mla_proj_kernel, tm=tm, tk=tk, pos0=pos0, seq_len=seq_len, q_rank=q_rank, kv_rank=kv_rank,
        rope_dim=rope_dim, nope_dim=nope_dim, scale=scale)
    out_specs = [hrow(kv_rank + rope_dim), row(kv_rank), row(rope_dim)]
    out_shape = [jax.ShapeDtypeStruct((heads, n, kv_rank + rope_dim), BF16),
                 jax.ShapeDtypeStruct((n, kv_rank), F32),
                 jax.ShapeDtypeStruct((n, rope_dim), F32)]
    assert seq_len == 1 or seq_len % tm == 0, "a row tile must not straddle two sequences"
    if tk:
        assert tm % tk == 0
        out_specs +=[row(kv_rank + rope_dim), pl.BlockSpec((tm // tk, kv_rank, tk), lambda i: (i, 0, 0))]
        out_shape += [jax.ShapeDtypeStruct((n, kv_rank + rope_dim), BF16),
                      jax.ShapeDtypeStruct((n // tk, kv_rank, tk), BF16)]
    return pl.pallas_call(
        kern,
        grid=(n // tm,),
        in_specs=[row(d), _const_spec(w_in_x.shape), _const_spec((1, q_rank)), _const_spec((1, kv_rank)),
                  _const_spec(w_uq_x.shape), _const_spec(w_uk_t.shape),
                  _const_spec((1, LANES)), _const_spec((1, LANES))],
        out_specs=out_specs,
        out_shape=out_shape,
        scratch_shapes=[] if seq_len == 1 else [pltpu.VMEM((tm, LANES), F32), pltpu.VMEM((tm, LANES), F32)],
        compiler_params=_params(1),
        name="mla_project",
    )(x, w_in_x, q_norm[None, :], kv_norm[None, :], w_uq_x, w_uk_t, inv_l, sgn)


def _prompt_attn_kernel(q_ref, k_ref, vt_ref, o_ref, m_sc, l_sc, acc_sc, *, tq, tk):
    heads, _, ck = q_ref.shape
    c = vt_ref.shape[1]
    qi = pl.program_id(1)
    q = q_ref[...].reshape(heads * tq, ck)
    m_sc[...] = jnp.full(m_sc.shape, NEG_INF, F32)
    l_sc[...] = jnp.zeros(l_sc.shape, F32)
    acc_sc[...] = jnp.zeros(acc_sc.shape, F32)

    def step(j, masked):
        k0 = pl.multiple_of(j * tk, tk)
        s = _dot_nt(k_ref[pl.ds(k0, tk), :], q)
        if masked:
            k_pos = k0 + lax.broadcasted_iota(jnp.int32, s.shape, 0)
            q_pos = qi * tq + (lax.broadcasted_iota(jnp.int32, s.shape, 1) & (tq - 1))
            s = jnp.where(k_pos <= q_pos, s, NEG_INF)
        m_prev = m_sc[...]
        m_new = jnp.maximum(m_prev, jnp.max(s, axis=0, keepdims=True))
        a = jnp.exp2(m_prev - m_new)
        p = jnp.exp2(s - m_new)
        l_sc[...] = a * l_sc[...] + jnp.sum(p, axis=0, keepdims=True)
        acc_sc[...] = a * acc_sc[...] + _dot(vt_ref[j], p.astype(BF16))
        m_sc[...] = m_new

    n_full = (qi * tq) // tk

    def body(jj, carry):
        step(2 * jj, False)
        step(2 * jj + 1, False)
        return carry

    lax.fori_loop(0, n_full // 2, body, 0)

    @pl.when(n_full % 2 == 1)
    def _():
        step(n_full - 1, False)

    step(n_full, True)

    out = (acc_sc[...] / l_sc[...]).T
    for h in range(heads):
        o_ref[:, h * c:(h + 1) * c] = out[h * tq:(h + 1) * tq, :].astype(o_ref.dtype)


def _prompt_attention(q, k, vt, *, batch, seq, tq, tk):
    heads, n, ck = q.shape
    c = vt.shape[1]
    assert tq & (tq - 1) == 0 and tk % tq == 0 and seq % tk == 0 and vt.shape == (n // tk, c, tk)
    nq = seq // tq
    nk = seq // tk
    return pl.pallas_call(
        functools.partial(_prompt_attn_kernel, tq=tq, tk=tk),
        grid=(batch, nq),
        in_specs=[pl.BlockSpec((heads, tq, ck), lambda b, i: (0, b * nq + i, 0)),
                  pl.BlockSpec((seq, ck), lambda b, i: (b, 0)),
                  pl.BlockSpec((nk, c, tk), lambda b, i: (b, 0, 0))],
        out_specs=pl.BlockSpec((tq, heads * c), lambda b, i: (b * nq + i, 0)),
        out_shape=jax.ShapeDtypeStruct((n, heads * c), BF16),
        scratch_shapes=[pltpu.VMEM((1, heads * tq), F32), pltpu.VMEM((1, heads * tq), F32),
                        pltpu.VMEM((c, heads * tq), F32)],
        compiler_params=_params(2),
        name="prompt_attention",
    )(q, k, vt)


def _sample_attn_kernel(pt_ref, q_ref, cnew_ref, rnew_ref, ckv_hbm, krt_hbm, o_ref,
                        kbuf, rbuf, sem, *, layer, n_pages, page, n_chunks):
    b = pl.program_id(0)
    nb = pl.num_programs(0)
    slot = b % 2

    def page_copies(row, dst_slot, p):
        pg = pt_ref[row * n_pages + p]
        keys = pl.ds(p * page, page)
        return (pltpu.make_async_copy(ckv_hbm.at[layer, pg], kbuf.at[dst_slot, keys], sem.at[dst_slot, 0]),
                pltpu.make_async_copy(krt_hbm.at[layer, pg], rbuf.at[dst_slot, :, keys], sem.at[dst_slot, 1]))

    def start_row(row, dst_slot):
        for p in range(n_pages):
            for cp in page_copies(row, dst_slot, p):
                cp.start()

    @pl.when(b == 0)
    def _():
        start_row(0, 0)

    @pl.when(b + 1 < nb)
    def _():
        start_row(b + 1, 1 - slot)

    for p in range(n_pages):
        for cp in page_copies(b, slot, p):
            cp.wait()

    ql = q_ref[0, :, :kbuf.shape[2]]
    qr = q_ref[0, :, kbuf.shape[2]:]
    c_new = cnew_ref[0]
    r_new = rnew_ref[0]
    m = (jnp.sum(ql.astype(F32) * c_new, axis=-1, keepdims=True)
         + jnp.sum(qr.astype(F32) * r_new, axis=-1, keepdims=True))
    l = jnp.ones_like(m)
    acc = jnp.broadcast_to(c_new, (ql.shape[0], c_new.shape[1]))
    ck = (n_pages * page) // n_chunks
    for c in range(n_chunks):
        keys = pl.ds(c * ck, ck)
        kc = kbuf[slot, keys, :].astype(BF16)
        s = _dot_nt(ql, kc) + _dot(qr, rbuf[slot, :, keys].astype(BF16))
        m_new = jnp.maximum(m, jnp.max(s, axis=-1, keepdims=True))
        a = jnp.exp2(m - m_new)
        p = jnp.exp2(s - m_new)
        l = a * l + jnp.sum(p, axis=-1, keepdims=True)
        acc = a * acc + _dot(p.astype(BF16), kc)
        m = m_new
    o_ref[0] = (acc / l).astype(o_ref.dtype)


def _sample_attention(page_table, q, c_new, r_new, cache_kv, cache_kr_t, *, layer):
    db, heads, _ = q.shape
    c, r = c_new.shape[2], r_new.shape[2]
    assert q.shape[2] == c + r
    n_pages = page_table.shape[1]
    page = cache_kv.shape[2]
    assert cache_kr_t.shape[2:] == (r, page)
    past = n_pages * page
    n_chunks = max(1, past // DECODE_CHUNK)
    blk = lambda h, w: pl.BlockSpec((1, h, w), lambda b, pt: (b, 0, 0))
    grid_spec = pltpu.PrefetchScalarGridSpec(
        num_scalar_prefetch=1,
        grid=(db,),
        in_specs=[blk(heads, c + r), blk(1, c), blk(1, r),
                  pl.BlockSpec(memory_space=pl.ANY), pl.BlockSpec(memory_space=pl.ANY)],
        out_specs=blk(heads, c),
        scratch_shapes=[pltpu.VMEM((2, past, c), F32), pltpu.VMEM((2, r, past), F32),
                        pltpu.SemaphoreType.DMA((2, 2))],
    )
    return pl.pallas_call(
        functools.partial(_sample_attn_kernel, layer=layer, n_pages=n_pages, page=page, n_chunks=n_chunks),
        grid_spec=grid_spec,
        out_shape=jax.ShapeDtypeStruct((db, heads, c), BF16),
        compiler_params=_params(1),
        name="sample_attention",
    )(page_table.reshape(-1), q, c_new, r_new, cache_kv, cache_kr_t)


def _mla_out_kernel(o_ref, x_ref, wuv_ref, wo_ref, g_ref, b_ref, y_ref, *, alpha):
    heads, c, _ = wuv_ref.shape
    parts = [_dot(o_ref[:, h * c:(h + 1) * c], wuv_ref[h]).astype(BF16) for h in range(heads)]
    mix = _dot(jnp.concatenate(parts, axis=-1), wo_ref[...])
    y_ref[...] = _layernorm(alpha * x_ref[...] + mix, g_ref[...], b_ref[...])


def _mla_out_post(o_lat, x, w_uv_h, w_o, g, b, *, alpha, tm):
    n, d = x.shape
    row = lambda w: pl.BlockSpec((tm, w), lambda i: (i, 0))
    return pl.pallas_call(
        functools.partial(_mla_out_kernel, alpha=alpha),
        grid=(n // tm,),
        in_specs=[row(o_lat.shape[1]), row(d), _const_spec(w_uv_h.shape), _const_spec(w_o.shape),
                  _const_spec((1, d)), _const_spec((1, d))],
        out_specs=row(d),
        out_shape=jax.ShapeDtypeStruct((n, d), F32),
        compiler_params=_params(1),
        name="mla_out_post",
    )(o_lat, x, w_uv_h, w_o, g, b)


def _cmlp_kernel(*refs, alpha, tm, heads, sample):
    if sample:
        (x_ref, win_ref, bin_ref, vg_ref, vb_ref, wcol_ref, bcol_ref, wout_ref, g_ref, b_ref,
         y_ref, v_ref) = refs
    else:
        (x_ref, win_ref, bin_ref, vg_ref, vb_ref, ws_ref, bs_ref, wout_ref, g_ref, b_ref,
         y_ref, vn_sc) = refs
    dc = wout_ref.shape[0]
    hd = dc // heads
    x = x_ref[...]
    xb = x.astype(BF16)
    v = jax.nn.gelu(_dot(xb, win_ref[:, dc:]) + bin_ref[:, dc:])
    vn = _layernorm(v, vg_ref[...], vb_ref[...])
    if sample:
        v_ref[...] = vn
        mixed = vn * wcol_ref[...] + bcol_ref[...]
        u = jax.nn.gelu(_dot(xb, win_ref[:, :dc]) + bin_ref[:, :dc])
        mix = _dot((u * mixed).astype(BF16), wout_ref[...])
    else:
        vn_sc[...] = vn.astype(BF16)
        ri = lax.broadcasted_iota(jnp.int32, (CHUNK, CHUNK), 0)
        ci = lax.broadcasted_iota(jnp.int32, (CHUNK, CHUNK), 1)
        mix = None
        group = next(k for k in range(1, heads + 1) if heads % k == 0 and (k * hd) % MXU_COLS == 0)
        for h0 in range(0, heads, group):
            parts = []
            for h in range(h0, h0 + group):
                cols = slice(h * hd, (h + 1) * hd)
                w_tril = jnp.where(ci <= ri, ws_ref[h], 0.0).astype(BF16)
                bias = bs_ref[h]
                parts.append(jnp.concatenate(
                    [_dot(w_tril, vn_sc[c * CHUNK:(c + 1) * CHUNK, cols]) + bias for c in range(tm // CHUNK)],
                    axis=0))
            mixed = jnp.concatenate(parts, axis=-1)
            gcols = slice(h0 * hd, (h0 + group) * hd)
            u = jax.nn.gelu(_dot(xb, win_ref[:, gcols]) + bin_ref[:, gcols])
            d = _dot((u * mixed).astype(BF16), wout_ref[gcols, :])
            mix = d if mix is None else mix + d
    y_ref[...] = _layernorm(alpha * x + mix, g_ref[...], b_ref[...])


def _chunk_mlp_post(x, w_in, b_in, v_g, v_b, w_s, b_s, w_out, g, b, *, alpha, tm, sample):
    n, d = x.shape
    dc = w_out.shape[0]
    heads = w_s.shape[0]
    row = pl.BlockSpec((tm, d), lambda i: (i, 0))
    common = [row, _const_spec(w_in.shape), _const_spec((1, 2 * dc)), _const_spec((1, dc)), _const_spec((1, dc))]
    tail = [_const_spec(w_out.shape), _const_spec((1, d)), _const_spec((1, d))]
    kern = functools.partial(_cmlp_kernel, alpha=alpha, tm=tm, heads=heads, sample=sample)
    y_shape = jax.ShapeDtypeStruct((n, d), F32)
    if sample:
        hd = dc // heads
        wcol = jnp.repeat(w_s[:, 0, 0], hd)[None, :]
        bcol = jnp.repeat(b_s[:, 0], hd)[None, :]
        return pl.pallas_call(
            kern, grid=(n // tm,),
            in_specs=common + [_const_spec((1, dc)), _const_spec((1, dc))] + tail,
            out_specs=[row, pl.BlockSpec((tm, dc), lambda i: (i, 0))],
            out_shape=[y_shape, jax.ShapeDtypeStruct((n, dc), F32)],
            compiler_params=_params(1), name="chunk_mlp_sample",
        )(x, w_in, b_in[None, :], v_g[None, :], v_b[None, :], wcol, bcol, w_out, g, b)
    assert tm % CHUNK == 0
    y = pl.pallas_call(
        kern, grid=(n // tm,),
        in_specs=common + [_const_spec(w_s.shape), _const_spec((heads, CHUNK, 1))] + tail,
        out_specs=row, out_shape=y_shape,
        scratch_shapes=[pltpu.VMEM((tm, dc), BF16)],
        compiler_params=_params(1), name="chunk_mlp_prompt",
    )(x, w_in, b_in[None, :], v_g[None, :], v_b[None, :], w_s, b_s[:, :, None], w_out, g, b)
    return y, None


def _pool_tail(pooled_groups, x, wgrp_ref, scale_ref, wout_ref, g_ref, b_ref, alpha):
    z = jnp.concatenate(
        [_dot(p.astype(BF16), wgrp_ref[gi]) for gi, p in enumerate(pooled_groups)], axis=-1)
    mix = _dot((z * scale_ref[...]).astype(BF16), wout_ref[...])
    return _layernorm(alpha * x + mix, g_ref[...], b_ref[...])


def _pool_prompt_kernel(x_ref, halo_ref, win_ref, wgrp_ref, scale_ref, wout_ref, g_ref, b_ref,
                        y_ref, last_ref, h_sc, *, alpha, tm, tiles_per_seq, halo):
    t0 = (pl.program_id(0) % tiles_per_seq) * tm
    x = x_ref[...]
    h = _dot(x.astype(BF16), win_ref[...])
    h_halo = _dot(halo_ref[...].astype(BF16), win_ref[...])
    h_sc[:halo, :] = jnp.where(t0 > 0, h_halo, 0.0)
    h_sc[halo:, :] = h
    last_ref[0] = h[tm - halo:, :]
    gd = h.shape[1] // len(POOL_WINDOWS)
    t = t0 + lax.broadcasted_iota(jnp.int32, (tm, 1), 0)
    pooled = []
    for gi, w in enumerate(POOL_WINDOWS):
        cols = slice(gi * gd, (gi + 1) * gd)
        tot = h[:, cols]
        for k in range(1, w):
            tot = tot + h_sc[halo - k:halo - k + tm, cols]
        cnt = jnp.minimum(t + 1, w).astype(F32)
        pooled.append(tot / cnt - h[:, cols])
    y_ref[...] = _pool_tail(pooled, x, wgrp_ref, scale_ref, wout_ref, g_ref, b_ref, alpha)


def _pool_prompt_post(x, w_in, w_grp, scale, w_out, g, b, *, alpha, tm, seq):
    n, d = x.shape
    dp = w_in.shape[1]
    halo = max(POOL_WINDOWS)
    assert seq % tm == 0 and tm % halo == 0 and halo % 8 == 0
    tiles_per_seq = seq // tm
    hb = tm // halo
    row = pl.BlockSpec((tm, d), lambda i: (i, 0))
    kern = functools.partial(_pool_prompt_kernel, alpha=alpha, tm=tm, tiles_per_seq=tiles_per_seq, halo=halo)
    return pl.pallas_call(
        kern, grid=(n // tm,),
        in_specs=[row, pl.BlockSpec((halo, d), lambda i: (jnp.maximum(i * hb - 1, 0), 0)),
                  _const_spec(w_in.shape), _const_spec(w_grp.shape), _const_spec((1, dp)),
                  _const_spec(w_out.shape), _const_spec((1, d)), _const_spec((1, d))],
        out_specs=[row, pl.BlockSpec((1, halo, dp), lambda i: (i // tiles_per_seq, 0, 0))],
        out_shape=[jax.ShapeDtypeStruct((n, d), F32),
                   jax.ShapeDtypeStruct((n // seq, halo, dp), F32)],
        scratch_shapes=[pltpu.VMEM((tm + halo, dp), F32)],
        compiler_params=_params(1), name="pool_prompt",
    )(x, x, w_in, w_grp, scale[None, :], w_out, g, b)


def _pool_sample_kernel(x_ref, st_ref, win_ref, wgrp_ref, scale_ref, wout_ref, g_ref, b_ref,
                        y_ref, h_ref, *, alpha):
    x = x_ref[...]
    h = _dot(x.astype(BF16), win_ref[...])
    h_ref[...] = h
    nbuf = st_ref.shape[0]
    gd = h.shape[1] // len(POOL_WINDOWS)
    pooled = []
    for gi, w in enumerate(POOL_WINDOWS):
        cols = slice(gi * gd, (gi + 1) * gd)
        tot = h[:, cols]
        for k in range(1, w):
            tot = tot + st_ref[nbuf - k, :, cols]
        pooled.append(tot / float(w) - h[:, cols])
    y_ref[...] = _pool_tail(pooled, x, wgrp_ref, scale_ref, wout_ref, g_ref, b_ref, alpha)


def _pool_sample_post(x, state, w_in, w_grp, scale, w_out, g, b, *, alpha):
    n, d = x.shape
    dp = w_in.shape[1]
    assert state.shape[0] == max(POOL_WINDOWS) - 1
    return pl.pallas_call(
        functools.partial(_pool_sample_kernel, alpha=alpha), grid=(1,),
        in_specs=[_const_spec((n, d)), _const_spec(state.shape), _const_spec(w_in.shape),
                  _const_spec(w_grp.shape), _const_spec((1, dp)), _const_spec(w_out.shape),
                  _const_spec((1, d)), _const_spec((1, d))],
        out_specs=[pl.BlockSpec((n, d), lambda i: (0, 0)), pl.BlockSpec((n, dp), lambda i: (0, 0))],
        out_shape=[jax.ShapeDtypeStruct((n, d), F32), jax.ShapeDtypeStruct((n, dp), F32)],
        compiler_params=_params(1), name="pool_sample",
    )(x, state, w_in, w_grp, scale[None, :], w_out, g, b)


def _row_tile(n, want):
    tm = min(n, want)
    assert n % tm == 0
    return tm


def kernel(x_prompt, x_sample, cache_kv_latent, cache_k_rope, state_pool, page_table, ln_g, ln_b, ffn_w_gate, ffn_w_up, ffn_w_down, a_w_in, a_q_norm, a_kv_norm, a_w_uq, a_w_uk, a_w_uv, a_w_o, b_w_in, b_b_in, b_v_norm_g, b_v_norm_b, b_w_s, b_b_s, b_w_out, c_w_in, c_w_grp, c_scale, c_w_out):
    batch, seq, d = x_prompt.shape
    db, t_new, _ = x_sample.shape
    assert t_new == 1, "decode attention handles one new token per sample row"
    depth = ln_g.shape[0]
    alpha = (2 * depth) ** 0.25
    kv_rank, heads, nope_dim = a_w_uk.shape[1:]
    rope_dim = cache_k_rope.shape[-1]
    q_rank = a_w_uq.shape[1]
    scale = (nope_dim + rope_dim) ** -0.5 * math.log2(math.e)
    past_len = page_table.shape[1] * cache_kv_latent.shape[2]
    cmlp_heads = b_w_s.shape[1]
    n_p, n_s = batch * seq, db * t_new
    tm_p, tm_s = _row_tile(n_p, 512), _row_tile(n_s, 512)

    y_p = x_prompt.reshape(n_p, d)
    y_s = x_sample.reshape(n_s, d)
    wgu, wd = _ffn_gate_up(ffn_w_gate, ffn_w_up), ffn_w_down.astype(BF16)
    cache_kr_t = jnp.swapaxes(cache_k_rope, 2, 3)
    outs = {k: [] for k in ("ckv_p", "kr_p", "ckv_s", "kr_s", "v_s", "pool_p", "pool_s")}

    def ffn(y, tm, i, k, ln_idx):
        tm_ffn = _row_tile(y.shape[0], FFN_TILE)
        return _ffn_post(y, wgu, wd, ln_g[i, ln_idx][None, :], ln_b[i, ln_idx][None, :],
                         layer=i, half=k, alpha=alpha, tm=tm_ffn, sub=_row_tile(tm_ffn, FFN_SUB))

    for i in range(depth):
        kind, j = i % 3, i // 3
        y_p = ffn(y_p, tm_p, i, 0, 0)
        y_s = ffn(y_s, tm_s, i, 0, 0)
        g1, b1 = ln_g[i, 1][None, :], ln_b[i, 1][None, :]

        if kind == 0:
            w_in_x, w_uq_x, w_uk_t, w_uv_h, w_o = _mla_weights(
                a_w_in[j], a_w_uq[j], a_w_uk[j], a_w_uv[j], a_w_o[j],
                q_rank=q_rank, kv_rank=kv_rank, rope_dim=rope_dim, heads=heads, nope_dim=nope_dim)
            proj = functools.partial(_mla_project, w_in_x=w_in_x, q_norm=a_q_norm[j], kv_norm=a_kv_norm[j],
                                     w_uq_x=w_uq_x, w_uk_t=w_uk_t, rope_dim=rope_dim, scale=scale)
            q, ckv, kr, k_cat, vt = proj(y_p, tm=tm_p, tk=ATTN_TK, pos0=0, seq_len=seq)
            o_lat = _prompt_attention(q, k_cat, vt, batch=batch, seq=seq, tq=ATTN_TQ, tk=ATTN_TK)
            y_p = _mla_out_post(o_lat, y_p, w_uv_h, w_o, g1, b1, alpha=alpha, tm=tm_p)
            outs["ckv_p"].append(ckv.reshape(batch, seq, kv_rank))
            outs["kr_p"].append(kr.reshape(batch, seq, rope_dim))

            q, ckv, kr = proj(y_s, tm=tm_s, tk=0, pos0=past_len, seq_len=t_new)
            o_s = _sample_attention(page_table, jnp.transpose(q, (1, 0, 2)),
                                    ckv[:, None, :], kr[:, None, :], cache_kv_latent, cache_kr_t, layer=j)
            y_s = _mla_out_post(o_s.reshape(n_s, heads * kv_rank), y_s, w_uv_h, w_o, g1, b1, alpha=alpha, tm=tm_s)
            outs["ckv_s"].append(ckv.reshape(db, t_new, kv_rank))
            outs["kr_s"].append(kr.reshape(db, t_new, rope_dim))
        elif kind == 1:
            cm = functools.partial(_chunk_mlp_post, w_in=b_w_in[j].astype(BF16), b_in=b_b_in[j], v_g=b_v_norm_g[j],
                                   v_b=b_v_norm_b[j], w_s=b_w_s[j], b_s=b_b_s[j], w_out=b_w_out[j].astype(BF16),
                                   g=g1, b=b1, alpha=alpha)
            y_p, _ = cm(y_p, tm=tm_p, sample=False)
            y_s, v_rows = cm(y_s, tm=tm_s, sample=True)
            outs["v_s"].append(v_rows.reshape(db, t_new, -1))
        else:
            w_in, w_grp, w_out = c_w_in[j].astype(BF16), c_w_grp[j].astype(BF16), c_w_out[j].astype(BF16)
            y_p, last = _pool_prompt_post(y_p, w_in, w_grp, c_scale[j], w_out, g1, b1, alpha=alpha, tm=tm_p, seq=seq)
            nbuf = state_pool.shape[2]
            outs["pool_p"].append(last[:, last.shape[1] - nbuf:, :])
            y_s, h_new = _pool_sample_post(y_s, jnp.transpose(state_pool[j], (1, 0, 2)), w_in, w_grp, c_scale[j], w_out, g1, b1, alpha=alpha)
            outs["pool_s"].append(jnp.concatenate([state_pool[j][:, 1:], h_new[:, None, :]], axis=1))

        y_p = ffn(y_p, tm_p, i, 1, 2)
        y_s = ffn(y_s, tm_s, i, 1, 2)

    return (y_p.reshape(batch, seq, d), y_s.reshape(db, t_new, d),
            jnp.stack(outs["ckv_p"]), jnp.stack(outs["kr_p"]), jnp.stack(outs["ckv_s"]), jnp.stack(outs["kr_s"]),
            jnp.stack(outs["v_s"]), jnp.stack(outs["pool_p"]), jnp.stack(outs["pool_s"]))
```

```python
import functools
import math

import jax
import jax.numpy as jnp
from jax import lax
from jax.experimental import pallas as pl
from jax.experimental.pallas import tpu as pltpu

F32 = jnp.float32
BF16 = jnp.bfloat16

LN_EPS = 1e-5
RMS_EPS = 1e-6
NEG_INF = -1e30
ROPE_THETA = 10000.0
POOL_WINDOWS = (2, 4, 8, 16)
CHUNK = 128

V7X_VMEM_BYTES = 64 * 1024 * 1024
VMEM_LIMIT = V7X_VMEM_BYTES - 8 * 1024 * 1024
LANES = 128
MXU_COLS = 256
ATTN_TQ = 256
ATTN_TK = 256
ATTN_WIDE = 2
DECODE_CHUNK = 8192
ROW_TILE = 1024
ROW_SUB = 256
FFN_CHUNKS = 3


def _params(n_grid_dims):
    return pltpu.CompilerParams(
        dimension_semantics=("arbitrary",) * n_grid_dims, vmem_limit_bytes=VMEM_LIMIT)


def _const_spec(shape):
    nd = len(shape)
    return pl.BlockSpec(shape, lambda *_: (0,) * nd, pipeline_mode=pl.Buffered(1))


def _dot(a, b):
    return jnp.dot(a, b, preferred_element_type=F32)


def _dot_nt(a, b):
    return lax.dot_general(a, b, (((1,), (1,)), ((), ())), preferred_element_type=F32)


def _layernorm(x, g, b):
    mu = jnp.mean(x, axis=-1, keepdims=True)
    xc = x - mu
    var = jnp.mean(xc * xc, axis=-1, keepdims=True)
    return xc * lax.rsqrt(var + LN_EPS) * g + b


def _rmsnorm(x, g):
    return x * lax.rsqrt(jnp.mean(x * x, axis=-1, keepdims=True) + RMS_EPS) * g


def _ffn_kernel(x_ref, wgu_ref, wd_ref, g_ref, b_ref, o_ref, *, alpha, n_chunks, sub):
    w = wd_ref.shape[0] // n_chunks
    for r in range(x_ref.shape[0] // sub):
        rows = slice(r * sub, (r + 1) * sub)
        x = x_ref[rows, :]
        xb = x.astype(BF16)
        hs = []
        for c in range(n_chunks):
            gu = _dot(xb, wgu_ref[:, 2 * w * c:2 * w * (c + 1)])
            gate, up = gu[:, :w], gu[:, w:]
            hs.append((gate * jax.nn.sigmoid(gate) * up).astype(BF16))
        h = hs[0] if n_chunks == 1 else jnp.concatenate(hs, axis=1)
        o_ref[rows, :] = _layernorm(alpha * x + 0.5 * _dot(h, wd_ref[...]), g_ref[...], b_ref[...])


def _gate_up_kernel(wg_ref, wu_ref, o_ref, *, n_chunks):
    w = wg_ref.shape[-1] // n_chunks
    for c in range(n_chunks):
        o_ref[:, 2 * w * c:2 * w * c + w] = wg_ref[:, w * c:w * (c + 1)].astype(BF16)
        o_ref[:, 2 * w * c + w:2 * w * (c + 1)] = wu_ref[:, w * c:w * (c + 1)].astype(BF16)


def _ffn_gate_up(wg, wu, n_chunks, rows=256):
    n_l, n_h, d, f = wg.shape
    w = f // n_chunks
    assert f % n_chunks == 0 and w % LANES == 0 and (2 * w) % MXU_COLS == 0 and d % rows == 0
    blk = lambda m: pl.BlockSpec((None, None, rows, m), lambda l, h, r: (l, h, r, 0))
    return pl.pallas_call(
        functools.partial(_gate_up_kernel, n_chunks=n_chunks),
        grid=(n_l, n_h, d // rows),
        in_specs=[blk(f), blk(f)],
        out_specs=blk(2 * f),
        out_shape=jax.ShapeDtypeStruct((n_l, n_h, d, 2 * f), BF16),
        compiler_params=_params(3),
        name="ffn_gate_up_bf16",
    )(wg, wu)


def _ffn_post(x, wgu, wd, g, b, *, layer, half, alpha, tm, sub, n_chunks):
    n, d = x.shape
    f = wd.shape[-2]
    assert tm % sub == 0
    row = pl.BlockSpec((tm, d), lambda i: (i, 0))

    def weight(k, m):
        return pl.BlockSpec((None, None, k, m), lambda i: (layer, half, 0, 0), pipeline_mode=pl.Buffered(1))

    return pl.pallas_call(
        functools.partial(_ffn_kernel, alpha=alpha, n_chunks=n_chunks, sub=sub),
        grid=(n // tm,),
        in_specs=[row, weight(d, 2 * f), weight(f, d), _const_spec((1, d)), _const_spec((1, d))],
        out_specs=row,
        out_shape=jax.ShapeDtypeStruct((n, d), F32),
        compiler_params=_params(1),
        name="ffn_post",
    )(x, wgu, wd, g, b)


def _mla_proj_kernel(x_ref, win_ref, qn_ref, kvn_ref, wuq_ref, wukt_ref, inv_ref, sgn_ref,
                     qcat_ref, ckv_ref, kr_ref, *rest,
                     tm, tk, pos0, seq_len, q_rank, kv_rank, rope_dim, nope_dim, scale):
    key_refs, rope_sc = (rest[:2], rest[2:]) if tk else ((), rest)
    heads = qcat_ref.shape[0]
    xb = x_ref[...].astype(BF16)
    lat = _dot(xb, win_ref[...])
    c_q = _rmsnorm(lat[:, :q_rank], qn_ref[...])
    c_kv = _rmsnorm(lat[:, q_rank:q_rank + kv_rank], kvn_ref[...])
    o_r = q_rank + kv_rank
    kr_raw = lat[:, o_r:o_r + rope_dim]
    kr_swp = lat[:, o_r + LANES:o_r + LANES + rope_dim]

    if seq_len == 1:
        ang = jnp.full((1, 1), pos0, F32) * inv_ref[...]
        cos2 = jnp.cos(ang)
        sin2 = jnp.sin(ang) * sgn_ref[...]
    else:
        cos_off, sin_off = rope_sc

        @pl.when(pl.program_id(0) == 0)
        def _():
            off = lax.broadcasted_iota(jnp.int32, (tm, 1), 0).astype(F32) * inv_ref[...]
            cos_off[...] = jnp.cos(off)
            sin_off[...] = jnp.sin(off)

        start = pos0 + (pl.program_id(0) * tm) % seq_len
        ang0 = jnp.full((1, 1), start, jnp.int32).astype(F32) * inv_ref[...]
        cos0, sin0 = jnp.cos(ang0), jnp.sin(ang0)
        cos2 = cos0 * cos_off[...] - sin0 * sin_off[...]
        sin2 = (sin0 * cos_off[...] + cos0 * sin_off[...]) * sgn_ref[...]

    k_rope = kr_raw * cos2[:, :rope_dim] + kr_swp * sin2[:, :rope_dim]
    ckv_ref[...] = c_kv
    kr_ref[...] = k_rope
    if tk:
        kcat_ref, vt_ref = key_refs
        kcat_ref[:, :kv_rank] = c_kv.astype(BF16)
        kcat_ref[:, kv_rank:] = k_rope.astype(BF16)
        for blk in range(tm // tk):
            vt_ref[blk] = c_kv[blk * tk:(blk + 1) * tk, :].T.astype(BF16)

    q = _dot(c_q.astype(BF16), wuq_ref[...])
    n_nope = heads * nope_dim
    n_rope = heads * rope_dim
    reps = n_rope // LANES
    cos_h = jnp.concatenate([cos2] * reps, axis=-1)
    sin_h = jnp.concatenate([sin2] * reps, axis=-1)
    q_rope = (q[:, n_nope:n_nope + n_rope] * cos_h + q[:, n_nope + n_rope:] * sin_h) * scale
    for h in range(heads):
        q_nope = q[:, h * nope_dim:(h + 1) * nope_dim].astype(BF16)
        qcat_ref[h, :, :kv_rank] = (_dot(q_nope, wukt_ref[h]) * scale).astype(BF16)
        qcat_ref[h, :, kv_rank:] = q_rope[:, h * rope_dim:(h + 1) * rope_dim].astype(BF16)


def _mla_weights(w_in, w_uq, w_uk, w_uv, w_o, *, q_rank, kv_rank, rope_dim, heads, nope_dim):
    half = rope_dim // 2
    d = w_in.shape[0]

    def swap(w):
        return jnp.concatenate([w[..., half:], w[..., :half]], axis=-1)

    w_r = w_in[:, q_rank + kv_rank:]
    zpad = jnp.zeros((d, LANES - rope_dim), w_in.dtype)
    w_in_x = jnp.concatenate([w_in[:, :q_rank + kv_rank], w_r, zpad, swap(w_r), zpad], axis=1)
    uq = w_uq.reshape(q_rank, heads, nope_dim + rope_dim)
    uq_n = uq[:, :, :nope_dim].reshape(q_rank, heads * nope_dim)
    uq_r = uq[:, :, nope_dim:]
    w_uq_x = jnp.concatenate([uq_n, uq_r.reshape(q_rank, heads * rope_dim),
                              swap(uq_r).reshape(q_rank, heads * rope_dim)], axis=1)
    w_uk_t = jnp.transpose(w_uk, (1, 2, 0))
    w_uv_h = jnp.transpose(w_uv, (1, 0, 2))
    return (w_in_x.astype(BF16), w_uq_x.astype(BF16), w_uk_t.astype(BF16),
            w_uv_h.astype(BF16), w_o.astype(BF16))


def _rope_tables(rope_dim):
    half = rope_dim // 2
    inv = ROPE_THETA ** (-jnp.arange(half, dtype=F32) / half)
    reps = LANES // half
    inv_l = jnp.tile(inv, reps)[None, :]
    sgn = jnp.tile(jnp.concatenate([-jnp.ones((half,), F32), jnp.ones((half,), F32)]), reps // 2)[None, :]
    return inv_l, sgn


def _mla_project(x, w_in_x, q_norm, kv_norm, w_uq_x, w_uk_t, *, tm, tk, pos0, seq_len, rope_dim, scale):
    n, d = x.shape
    heads, nope_dim, kv_rank = w_uk_t.shape
    q_rank = w_uq_x.shape[0]
    inv_l, sgn = _rope_tables(rope_dim)
    row = lambda w: pl.BlockSpec((tm, w), lambda i: (i, 0))
    hrow = lambda w: pl.BlockSpec((heads, tm, w), lambda i: (0, i, 0))
    kern = functools.partial(
        _mla_proj_kernel, tm=tm, tk=tk, pos0=pos0, seq_len=seq_len, q_rank=q_rank, kv_rank=kv_rank,
        rope_dim=rope_dim, nope_dim=nope_dim, scale=scale)
    out_specs = [hrow(kv_rank + rope_dim), row(kv_rank), row(rope_dim)]
    out_shape = [jax.ShapeDtypeStruct((heads, n, kv_rank + rope_dim), BF16),
                 jax.ShapeDtypeStruct((n, kv_rank), F32),
                 jax.ShapeDtypeStruct((n, rope_dim), F32)]
    assert seq_len == 1 or seq_len % tm == 0, "a row tile must not straddle two sequences"
    if tk:
        assert tm % tk == 0
        out_specs +=[row(kv_rank + rope_dim), pl.BlockSpec((tm // tk, kv_rank, tk), lambda i: (i, 0, 0))]
        out_shape += [jax.ShapeDtypeStruct((n, kv_rank + rope_dim), BF16),
                      jax.ShapeDtypeStruct((n // tk, kv_rank, tk), BF16)]
    return pl.pallas_call(
        kern,
        grid=(n // tm,),
        in_specs=[row(d), _const_spec(w_in_x.shape), _const_spec((1, q_rank)), _const_spec((1, kv_rank)),
                  _const_spec(w_uq_x.shape), _const_spec(w_uk_t.shape),
                  _const_spec((1, LANES)), _const_spec((1, LANES))],
        out_specs=out_specs,
        out_shape=out_shape,
        scratch_shapes=[] if seq_len == 1 else [pltpu.VMEM((tm, LANES), F32), pltpu.VMEM((tm, LANES), F32)],
        compiler_params=_params(1),
        name="mla_project",
    )(x, w_in_x, q_norm[None, :], kv_norm[None, :], w_uq_x, w_uk_t, inv_l, sgn)


def _prompt_attn_kernel(q_ref, k_ref, vt_ref, o_ref, m_sc, l_sc, acc_sc, *, tq, tk):
    heads, _, ck = q_ref.shape
    c = vt_ref.shape[1]
    qi = pl.program_id(1)
    q = q_ref[...].reshape(heads * tq, ck)

    def scores(j, n_blk):
        k0 = pl.multiple_of(j * tk, tk)
        return k0, _dot_nt(k_ref[pl.ds(k0, n_blk * tk), :], q)

    def weighted_values(j, n_blk, p):
        pv = _dot(vt_ref[j], p[:tk])
        for i in range(1, n_blk):
            pv = pv + _dot(vt_ref[j + i], p[i * tk:(i + 1) * tk])
        return pv

    n_full = (qi * tq) // tk
    k0, s = scores(n_full, 1)
    k_pos = k0 + lax.broadcasted_iota(jnp.int32, s.shape, 0)
    q_pos = qi * tq + (lax.broadcasted_iota(jnp.int32, s.shape, 1) & (tq - 1))
    s = jnp.where(k_pos <= q_pos, s, NEG_INF)
    m0 = jnp.max(s, axis=0, keepdims=True)
    p = jnp.exp2(s - m0)
    m_sc[...] = m0
    l_sc[...] = jnp.sum(p, axis=0, keepdims=True)
    acc_sc[...] = weighted_values(n_full, 1, p.astype(BF16))

    def step(j, n_blk):
        _, s = scores(j, n_blk)
        m_prev = m_sc[...]
        m_new = jnp.maximum(m_prev, jnp.max(s, axis=0, keepdims=True))
        a = jnp.exp2(m_prev - m_new)
        p = jnp.exp2(s - m_new)
        l_sc[...] = a * l_sc[...] + jnp.sum(p, axis=0, keepdims=True)
        acc_sc[...] = a * acc_sc[...] + weighted_values(j, n_blk, p.astype(BF16))
        m_sc[...] = m_new

    def body(jj, carry):
        step(ATTN_WIDE * jj, ATTN_WIDE)
        return carry

    lax.fori_loop(0, n_full // ATTN_WIDE, body, 0)

    def rest(j, carry):
        step(j, 1)
        return carry

    lax.fori_loop(n_full - n_full % ATTN_WIDE, n_full, rest, 0)

    out = (acc_sc[...] * (1.0 / l_sc[...])).T
    for h in range(heads):
        o_ref[:, h * c:(h + 1) * c] = out[h * tq:(h + 1) * tq, :].astype(o_ref.dtype)


def _prompt_attention(q, k, vt, *, batch, seq, tq, tk):
    heads, n, ck = q.shape
    c = vt.shape[1]
    assert tq & (tq - 1) == 0 and tk % tq == 0 and seq % tk == 0 and vt.shape == (n // tk, c, tk)
    nq = seq // tq
    nk = seq // tk
    return pl.pallas_call(
        functools.partial(_prompt_attn_kernel, tq=tq, tk=tk),
        grid=(batch, nq),
        in_specs=[pl.BlockSpec((heads, tq, ck), lambda b, i: (0, b * nq + i, 0)),
                  pl.BlockSpec((seq, ck), lambda b, i: (b, 0)),
                  pl.BlockSpec((nk, c, tk), lambda b, i: (b, 0, 0))],
        out_specs=pl.BlockSpec((tq, heads * c), lambda b, i: (b * nq + i, 0)),
        out_shape=jax.ShapeDtypeStruct((n, heads * c), BF16),
        scratch_shapes=[pltpu.VMEM((1, heads * tq), F32), pltpu.VMEM((1, heads * tq), F32),
                        pltpu.VMEM((c, heads * tq), F32)],
        compiler_params=_params(2),
        name="prompt_attention",
    )(q, k, vt)


def _sample_attn_kernel(pt_ref, q_ref, cnew_ref, rnew_ref, ckv_hbm, krt_hbm, o_ref,
                        kbuf, rbuf, sem, *, layer, n_pages, page, n_chunks):
    b = pl.program_id(0)
    nb = pl.num_programs(0)
    slot = b % 2

    def page_copies(row, dst_slot, p):
        pg = pt_ref[row * n_pages + p]
        keys = pl.ds(p * page, page)
        return (pltpu.make_async_copy(ckv_hbm.at[layer, pg], kbuf.at[dst_slot, keys], sem.at[dst_slot, 0]),
                pltpu.make_async_copy(krt_hbm.at[layer, pg], rbuf.at[dst_slot, :, keys], sem.at[dst_slot, 1]))

    def start_row(row, dst_slot):
        for p in range(n_pages):
            for cp in page_copies(row, dst_slot, p):
                cp.start()

    @pl.when(b == 0)
    def _():
        start_row(0, 0)

    @pl.when(b + 1 < nb)
    def _():
        start_row(b + 1, 1 - slot)

    for p in range(n_pages):
        for cp in page_copies(b, slot, p):
            cp.wait()

    ql = q_ref[0, :, :kbuf.shape[2]]
    qr = q_ref[0, :, kbuf.shape[2]:]
    c_new = cnew_ref[0]
    r_new = rnew_ref[0]
    m = (jnp.sum(ql.astype(F32) * c_new, axis=-1, keepdims=True)
         + jnp.sum(qr.astype(F32) * r_new, axis=-1, keepdims=True))
    l = jnp.ones_like(m)
    acc = jnp.broadcast_to(c_new, (ql.shape[0], c_new.shape[1]))
    ck = (n_pages * page) // n_chunks
    for c in range(n_chunks):
        keys = pl.ds(c * ck, ck)
        kc = kbuf[slot, keys, :].astype(BF16)
        s = _dot_nt(ql, kc) + _dot(qr, rbuf[slot, :, keys].astype(BF16))
        m_new = jnp.maximum(m, jnp.max(s, axis=-1, keepdims=True))
        a = jnp.exp2(m - m_new)
        p = jnp.exp2(s - m_new)
        l = a * l + jnp.sum(p, axis=-1, keepdims=True)
        acc = a * acc + _dot(p.astype(BF16), kc)
        m = m_new
    o_ref[0] = (acc / l).astype(o_ref.dtype)


def _sample_attention(page_table, q, c_new, r_new, cache_kv, cache_kr_t, *, layer):
    db, heads, _ = q.shape
    c, r = c_new.shape[2], r_new.shape[2]
    assert q.shape[2] == c + r
    n_pages = page_table.shape[1]
    page = cache_kv.shape[2]
    assert cache_kr_t.shape[2:] == (r, page)
    past = n_pages * page
    n_chunks = max(1, past // DECODE_CHUNK)
    blk = lambda h, w: pl.BlockSpec((1, h, w), lambda b, pt: (b, 0, 0))
    grid_spec = pltpu.PrefetchScalarGridSpec(
        num_scalar_prefetch=1,
        grid=(db,),
        in_specs=[blk(heads, c + r), blk(1, c), blk(1, r),
                  pl.BlockSpec(memory_space=pl.ANY), pl.BlockSpec(memory_space=pl.ANY)],
        out_specs=blk(heads, c),
        scratch_shapes=[pltpu.VMEM((2, past, c), F32), pltpu.VMEM((2, r, past), F32),
                        pltpu.SemaphoreType.DMA((2, 2))],
    )
    return pl.pallas_call(
        functools.partial(_sample_attn_kernel, layer=layer, n_pages=n_pages, page=page, n_chunks=n_chunks),
        grid_spec=grid_spec,
        out_shape=jax.ShapeDtypeStruct((db, heads, c), BF16),
        compiler_params=_params(1),
        name="sample_attention",
    )(page_table.reshape(-1), q, c_new, r_new, cache_kv, cache_kr_t)


def _mla_out_kernel(o_ref, x_ref, wuv_ref, wo_ref, g_ref, b_ref, y_ref, *, alpha, sub):
    heads, c, _ = wuv_ref.shape
    for r in range(x_ref.shape[0] // sub):
        rows = slice(r * sub, (r + 1) * sub)
        parts = [_dot(o_ref[rows, h * c:(h + 1) * c], wuv_ref[h]).astype(BF16) for h in range(heads)]
        mix = _dot(jnp.concatenate(parts, axis=-1), wo_ref[...])
        y_ref[rows, :] = _layernorm(alpha * x_ref[rows, :] + mix, g_ref[...], b_ref[...])


def _mla_out_post(o_lat, x, w_uv_h, w_o, g, b, *, alpha, tm, sub):
    n, d = x.shape
    assert tm % sub == 0
    row = lambda w: pl.BlockSpec((tm, w), lambda i: (i, 0))
    return pl.pallas_call(
        functools.partial(_mla_out_kernel, alpha=alpha, sub=sub),
        grid=(n // tm,),
        in_specs=[row(o_lat.shape[1]), row(d), _const_spec(w_uv_h.shape), _const_spec(w_o.shape),
                  _const_spec((1, d)), _const_spec((1, d))],
        out_specs=row(d),
        out_shape=jax.ShapeDtypeStruct((n, d), F32),
        compiler_params=_params(1),
        name="mla_out_post",
    )(o_lat, x, w_uv_h, w_o, g, b)


def _cmlp_kernel(*refs, alpha, sub, heads, sample):
    if sample:
        (x_ref, win_ref, bin_ref, vg_ref, vb_ref, wcol_ref, bcol_ref, wout_ref, g_ref, b_ref,
         y_ref, v_ref) = refs
    else:
        x_ref, win_ref, bin_ref, vg_ref, vb_ref, ws_ref, bs_ref, wout_ref, g_ref, b_ref, y_ref = refs
        ri = lax.broadcasted_iota(jnp.int32, (CHUNK, CHUNK), 0)
        ci = lax.broadcasted_iota(jnp.int32, (CHUNK, CHUNK), 1)
        w_tril = [jnp.where(ci <= ri, ws_ref[h], 0.0).astype(BF16) for h in range(heads)]
    dc = wout_ref.shape[0]
    hd = dc // heads
    group = next(k for k in range(1, heads + 1) if heads % k == 0 and (k * hd) % MXU_COLS == 0)
    for r in range(x_ref.shape[0] // sub):
        rows = slice(r * sub, (r + 1) * sub)
        x = x_ref[rows, :]
        xb = x.astype(BF16)
        v = jax.nn.gelu(_dot(xb, win_ref[:, dc:]) + bin_ref[:, dc:])
        vn = _layernorm(v, vg_ref[...], vb_ref[...])
        if sample:
            v_ref[rows, :] = vn
            mixed = vn * wcol_ref[...] + bcol_ref[...]
            u = jax.nn.gelu(_dot(xb, win_ref[:, :dc]) + bin_ref[:, :dc])
            mix = _dot((u * mixed).astype(BF16), wout_ref[...])
        else:
            vn_b = vn.astype(BF16)
            mix = None
            for h0 in range(0, heads, group):
                parts = []
                for h in range(h0, h0 + group):
                    cols = slice(h * hd, (h + 1) * hd)
                    bias = bs_ref[h]
                    parts.append(jnp.concatenate(
                        [_dot(w_tril[h], vn_b[c * CHUNK:(c + 1) * CHUNK, cols]) + bias
                         for c in range(sub // CHUNK)], axis=0))
                mixed = jnp.concatenate(parts, axis=-1)
                gcols = slice(h0 * hd, (h0 + group) * hd)
                u = jax.nn.gelu(_dot(xb, win_ref[:, gcols]) + bin_ref[:, gcols])
                d = _dot((u * mixed).astype(BF16), wout_ref[gcols, :])
                mix = d if mix is None else mix + d
        y_ref[rows, :] = _layernorm(alpha * x + mix, g_ref[...], b_ref[...])


def _chunk_mlp_post(x, w_in, b_in, v_g, v_b, w_s, b_s, w_out, g, b, *, alpha, tm, sub, sample):
    n, d = x.shape
    dc = w_out.shape[0]
    heads = w_s.shape[0]
    assert tm % sub == 0
    row = pl.BlockSpec((tm, d), lambda i: (i, 0))
    common = [row, _const_spec(w_in.shape), _const_spec((1, 2 * dc)), _const_spec((1, dc)), _const_spec((1, dc))]
    tail = [_const_spec(w_out.shape), _const_spec((1, d)), _const_spec((1, d))]
    kern = functools.partial(_cmlp_kernel, alpha=alpha, sub=sub, heads=heads, sample=sample)
    y_shape = jax.ShapeDtypeStruct((n, d), F32)
    if sample:
        hd = dc // heads
        wcol = jnp.repeat(w_s[:, 0, 0], hd)[None, :]
        bcol = jnp.repeat(b_s[:, 0], hd)[None, :]
        return pl.pallas_call(
            kern, grid=(n // tm,),
            in_specs=common + [_const_spec((1, dc)), _const_spec((1, dc))] + tail,
            out_specs=[row, pl.BlockSpec((tm, dc), lambda i: (i, 0))],
            out_shape=[y_shape, jax.ShapeDtypeStruct((n, dc), F32)],
            compiler_params=_params(1), name="chunk_mlp_sample",
        )(x, w_in, b_in[None, :], v_g[None, :], v_b[None, :], wcol, bcol, w_out, g, b)
    assert sub % CHUNK == 0
    y = pl.pallas_call(
        kern, grid=(n // tm,),
        in_specs=common + [_const_spec(w_s.shape), _const_spec((heads, CHUNK, 1))] + tail,
        out_specs=row, out_shape=y_shape,
        compiler_params=_params(1), name="chunk_mlp_prompt",
    )(x, w_in, b_in[None, :], v_g[None, :], v_b[None, :], w_s, b_s[:, :, None], w_out, g, b)
    return y, None


def _pool_tail(pooled_groups, x, wgrp_ref, scale_ref, wout_ref, g_ref, b_ref, alpha):
    z = jnp.concatenate(
        [_dot(p.astype(BF16), wgrp_ref[gi]) for gi, p in enumerate(pooled_groups)], axis=-1)
    mix = _dot((z * scale_ref[...]).astype(BF16), wout_ref[...])
    return _layernorm(alpha * x + mix, g_ref[...], b_ref[...])


def _pool_prompt_kernel(x_ref, halo_ref, win_ref, wgrp_ref, scale_ref, wout_ref, g_ref, b_ref,
                        y_ref, last_ref, h_sc, *, alpha, tm, tiles_per_seq, halo):
    t0 = (pl.program_id(0) % tiles_per_seq) * tm
    x = x_ref[...]
    h = _dot(x.astype(BF16), win_ref[...])
    h_halo = _dot(halo_ref[...].astype(BF16), win_ref[...])
    h_sc[:halo, :] = jnp.where(t0 > 0, h_halo, 0.0)
    h_sc[halo:, :] = h
    last_ref[0] = h[tm - halo:, :]
    gd = h.shape[1] // len(POOL_WINDOWS)
    t = t0 + lax.broadcasted_iota(jnp.int32, (tm, 1), 0)
    pooled = []
    for gi, w in enumerate(POOL_WINDOWS):
        cols = slice(gi * gd, (gi + 1) * gd)
        tot = h[:, cols]
        for k in range(1, w):
            tot = tot + h_sc[halo - k:halo - k + tm, cols]
        cnt = jnp.minimum(t + 1, w).astype(F32)
        pooled.append(tot / cnt - h[:, cols])
    y_ref[...] = _pool_tail(pooled, x, wgrp_ref, scale_ref, wout_ref, g_ref, b_ref, alpha)


def _pool_prompt_post(x, w_in, w_grp, scale, w_out, g, b, *, alpha, tm, seq):
    n, d = x.shape
    dp = w_in.shape[1]
    halo = max(POOL_WINDOWS)
    assert seq % tm == 0 and tm % halo == 0 and halo % 8 == 0
    tiles_per_seq = seq // tm
    hb = tm // halo
    row = pl.BlockSpec((tm, d), lambda i: (i, 0))
    kern = functools.partial(_pool_prompt_kernel, alpha=alpha, tm=tm, tiles_per_seq=tiles_per_seq, halo=halo)
    return pl.pallas_call(
        kern, grid=(n // tm,),
        in_specs=[row, pl.BlockSpec((halo, d), lambda i: (jnp.maximum(i * hb - 1, 0), 0)),
                  _const_spec(w_in.shape), _const_spec(w_grp.shape), _const_spec((1, dp)),
                  _const_spec(w_out.shape), _const_spec((1, d)), _const_spec((1, d))],
        out_specs=[row, pl.BlockSpec((1, halo, dp), lambda i: (i // tiles_per_seq, 0, 0))],
        out_shape=[jax.ShapeDtypeStruct((n, d), F32),
                   jax.ShapeDtypeStruct((n // seq, halo, dp), F32)],
        scratch_shapes=[pltpu.VMEM((tm + halo, dp), F32)],
        compiler_params=_params(1), name="pool_prompt",
    )(x, x, w_in, w_grp, scale[None, :], w_out, g, b)


def _pool_sample_kernel(x_ref, st_ref, win_ref, wgrp_ref, scale_ref, wout_ref, g_ref, b_ref,
                        y_ref, h_ref, *, alpha):
    x = x_ref[...]
    h = _dot(x.astype(BF16), win_ref[...])
    h_ref[...] = h
    nbuf = st_ref.shape[0]
    gd = h.shape[1] // len(POOL_WINDOWS)
    pooled = []
    for gi, w in enumerate(POOL_WINDOWS):
        cols = slice(gi * gd, (gi + 1) * gd)
        tot = h[:, cols]
        for k in range(1, w):
            tot = tot + st_ref[nbuf - k, :, cols]
        pooled.append(tot / float(w) - h[:, cols])
    y_ref[...] = _pool_tail(pooled, x, wgrp_ref, scale_ref, wout_ref, g_ref, b_ref, alpha)


def _pool_sample_post(x, state, w_in, w_grp, scale, w_out, g, b, *, alpha):
    n, d = x.shape
    dp = w_in.shape[1]
    assert state.shape[0] == max(POOL_WINDOWS) - 1
    return pl.pallas_call(
        functools.partial(_pool_sample_kernel, alpha=alpha), grid=(1,),
        in_specs=[_const_spec((n, d)), _const_spec(state.shape), _const_spec(w_in.shape),
                  _const_spec(w_grp.shape), _const_spec((1, dp)), _const_spec(w_out.shape),
                  _const_spec((1, d)), _const_spec((1, d))],
        out_specs=[pl.BlockSpec((n, d), lambda i: (0, 0)), pl.BlockSpec((n, dp), lambda i: (0, 0))],
        out_shape=[jax.ShapeDtypeStruct((n, d), F32), jax.ShapeDtypeStruct((n, dp), F32)],
        compiler_params=_params(1), name="pool_sample",
    )(x, state, w_in, w_grp, scale[None, :], w_out, g, b)


def _row_tile(n, want):
    tm = min(n, want)
    assert n % tm == 0
    return tm


def kernel(x_prompt, x_sample, cache_kv_latent, cache_k_rope, state_pool, page_table, ln_g, ln_b, ffn_w_gate, ffn_w_up, ffn_w_down, a_w_in, a_q_norm, a_kv_norm, a_w_uq, a_w_uk, a_w_uv, a_w_o, b_w_in, b_b_in, b_v_norm_g, b_v_norm_b, b_w_s, b_b_s, b_w_out, c_w_in, c_w_grp, c_scale, c_w_out):
    batch, seq, d = x_prompt.shape
    db, t_new, _ = x_sample.shape
    assert t_new == 1, "decode attention handles one new token per sample row"
    depth = ln_g.shape[0]
    alpha = (2 * depth) ** 0.25
    kv_rank, heads, nope_dim = a_w_uk.shape[1:]
    rope_dim = cache_k_rope.shape[-1]
    q_rank = a_w_uq.shape[1]
    scale = (nope_dim + rope_dim) ** -0.5 * math.log2(math.e)
    past_len = page_table.shape[1] * cache_kv_latent.shape[2]
    cmlp_heads = b_w_s.shape[1]
    n_p, n_s = batch * seq, db * t_new
    tm_p, tm_s = _row_tile(n_p, 512), _row_tile(n_s, 512)

    def sub_tiled(y):
        tm = _row_tile(y.shape[0], ROW_TILE)
        return dict(tm=tm, sub=_row_tile(tm, ROW_SUB))

    y_p = x_prompt.reshape(n_p, d)
    y_s = x_sample.reshape(n_s, d)
    wgu, wd = _ffn_gate_up(ffn_w_gate, ffn_w_up, FFN_CHUNKS), ffn_w_down.astype(BF16)
    cache_kr_t = jnp.swapaxes(cache_k_rope, 2, 3)
    outs = {k: [] for k in ("ckv_p", "kr_p", "ckv_s", "kr_s", "v_s", "pool_p", "pool_s")}

    def ffn(y, tm, i, k, ln_idx):
        return _ffn_post(y, wgu, wd, ln_g[i, ln_idx][None, :], ln_b[i, ln_idx][None, :], layer=i, half=k,
                         alpha=alpha, n_chunks=FFN_CHUNKS, **sub_tiled(y))

    for i in range(depth):
        kind, j = i % 3, i // 3
        y_p = ffn(y_p, tm_p, i, 0, 0)
        y_s = ffn(y_s, tm_s, i, 0, 0)
        g1, b1 = ln_g[i, 1][None, :], ln_b[i, 1][None, :]

        if kind == 0:
            w_in_x, w_uq_x, w_uk_t, w_uv_h, w_o = _mla_weights(
                a_w_in[j], a_w_uq[j], a_w_uk[j], a_w_uv[j], a_w_o[j],
                q_rank=q_rank, kv_rank=kv_rank, rope_dim=rope_dim, heads=heads, nope_dim=nope_dim)
            proj = functools.partial(_mla_project, w_in_x=w_in_x, q_norm=a_q_norm[j], kv_norm=a_kv_norm[j],
                                     w_uq_x=w_uq_x, w_uk_t=w_uk_t, rope_dim=rope_dim, scale=scale)
            q, ckv, kr, k_cat, vt = proj(y_p, tm=tm_p, tk=ATTN_TK, pos0=0, seq_len=seq)
            o_lat = _prompt_attention(q, k_cat, vt, batch=batch, seq=seq, tq=ATTN_TQ, tk=ATTN_TK)
            y_p = _mla_out_post(o_lat, y_p, w_uv_h, w_o, g1, b1, alpha=alpha, **sub_tiled(y_p))
            outs["ckv_p"].append(ckv.reshape(batch, seq, kv_rank))
            outs["kr_p"].append(kr.reshape(batch, seq, rope_dim))

            q, ckv, kr = proj(y_s, tm=tm_s, tk=0, pos0=past_len, seq_len=t_new)
            o_s = _sample_attention(page_table, jnp.transpose(q, (1, 0, 2)),
                                    ckv[:, None, :], kr[:, None, :], cache_kv_latent, cache_kr_t, layer=j)
            y_s = _mla_out_post(o_s.reshape(n_s, heads * kv_rank), y_s, w_uv_h, w_o, g1, b1, alpha=alpha,
                                **sub_tiled(y_s))
            outs["ckv_s"].append(ckv.reshape(db, t_new, kv_rank))
            outs["kr_s"].append(kr.reshape(db, t_new, rope_dim))
        elif kind == 1:
            cm = functools.partial(_chunk_mlp_post, w_in=b_w_in[j].astype(BF16), b_in=b_b_in[j], v_g=b_v_norm_g[j],
                                   v_b=b_v_norm_b[j], w_s=b_w_s[j], b_s=b_b_s[j], w_out=b_w_out[j].astype(BF16),
                                   g=g1, b=b1, alpha=alpha)
            y_p, _ = cm(y_p, sample=False, **sub_tiled(y_p))
            y_s, v_rows = cm(y_s, sample=True, **sub_tiled(y_s))
            outs["v_s"].append(v_rows.reshape(db, t_new, -1))
        else:
            w_in, w_grp, w_out = c_w_in[j].astype(BF16), c_w_grp[j].astype(BF16), c_w_out[j].astype(BF16)
            y_p, last = _pool_prompt_post(y_p, w_in, w_grp, c_scale[j], w_out, g1, b1, alpha=alpha, tm=tm_p, seq=seq)
            nbuf = state_pool.shape[2]
            outs["pool_p"].append(last[:, last.shape[1] - nbuf:, :])
            y_s, h_new = _pool_sample_post(y_s, jnp.transpose(state_pool[j], (1, 0, 2)), w_in, w_grp, c_scale[j], w_out, g1, b1, alpha=alpha)
            outs["pool_s"].append(jnp.concatenate([state_pool[j][:, 1:], h_new[:, None, :]], axis=1))

        y_p = ffn(y_p, tm_p, i, 1, 2)
        y_s = ffn(y_s, tm_s, i, 1, 2)

    return (y_p.reshape(batch, seq, d), y_s.reshape(db, t_new, d),
            jnp.stack(outs["ckv_p"]), jnp.stack(outs["kr_p"]), jnp.stack(outs["ckv_s"]), jnp.stack(outs["kr_s"]),
            jnp.stack(outs["v_s"]), jnp.stack(outs["pool_p"]), jnp.stack(outs["pool_s"]))
```

```python
import functools
import math

import jax
import jax.numpy as jnp
from jax import lax
from jax.experimental import pallas as pl
from jax.experimental.pallas import tpu as pltpu

F32 = jnp.float32
BF16 = jnp.bfloat16

LN_EPS = 1e-5
RMS_EPS = 1e-6
NEG_INF = -1e30
ROPE_THETA = 10000.0
POOL_WINDOWS = (2, 4, 8, 16)
CHUNK = 128

V7X_VMEM_BYTES = 64 * 1024 * 1024
VMEM_LIMIT = V7X_VMEM_BYTES - 8 * 1024 * 1024
LANES = 128
MXU_COLS = 256
ATTN_TQ = 256
ATTN_TK = 256
ATTN_WIDE = 2
DECODE_CHUNK = 8192
SEQ_TILE = 512
ROW_TILE = 1024
ROW_SUB = 256
CMLP_GROUP_COLS = 768
FFN_CHUNKS = 3


def _params(n_grid_dims):
    return pltpu.CompilerParams(
        dimension_semantics=("arbitrary",) * n_grid_dims, vmem_limit_bytes=VMEM_LIMIT)


def _const_spec(shape):
    nd = len(shape)
    return pl.BlockSpec(shape, lambda *_: (0,) * nd, pipeline_mode=pl.Buffered(1))


def _dot(a, b):
    return jnp.dot(a, b, preferred_element_type=F32)


def _dot_nt(a, b):
    return lax.dot_general(a, b, (((1,), (1,)), ((), ())), preferred_element_type=F32)


def _layernorm(x, g, b):
    mu = jnp.mean(x, axis=-1, keepdims=True)
    xc = x - mu
    var = jnp.mean(xc * xc, axis=-1, keepdims=True)
    return xc * lax.rsqrt(var + LN_EPS) * g + b


def _gelu_tanh(x):
    c0 = math.sqrt(2.0 / math.pi)
    hx = 0.5 * x
    return hx + hx * jnp.tanh(x * (c0 + (c0 * 0.044715) * (x * x)))


def _rmsnorm(x, g):
    return x * lax.rsqrt(jnp.mean(x * x, axis=-1, keepdims=True) + RMS_EPS) * g


def _ffn_kernel(x_ref, wgu_ref, wd_ref, g_ref, b_ref, o_ref, *, alpha, n_chunks, sub):
    w = wd_ref.shape[0] // n_chunks
    for r in range(x_ref.shape[0] // sub):
        rows = slice(r * sub, (r + 1) * sub)
        x = x_ref[rows, :]
        xb = x.astype(BF16)
        hs = []
        for c in range(n_chunks):
            gu = _dot(xb, wgu_ref[:, 2 * w * c:2 * w * (c + 1)])
            gate, up = gu[:, :w], gu[:, w:]
            hs.append((gate * jax.nn.sigmoid(gate) * up).astype(BF16))
        h = hs[0] if n_chunks == 1 else jnp.concatenate(hs, axis=1)
        o_ref[rows, :] = _layernorm(alpha * x + 0.5 * _dot(h, wd_ref[...]), g_ref[...], b_ref[...])


def _gate_up_kernel(wg_ref, wu_ref, o_ref, *, n_chunks):
    w = wg_ref.shape[-1] // n_chunks
    for c in range(n_chunks):
        o_ref[:, 2 * w * c:2 * w * c + w] = wg_ref[:, w * c:w * (c + 1)].astype(BF16)
        o_ref[:, 2 * w * c + w:2 * w * (c + 1)] = wu_ref[:, w * c:w * (c + 1)].astype(BF16)


def _ffn_gate_up(wg, wu, n_chunks, rows=256):
    n_l, n_h, d, f = wg.shape
    w = f // n_chunks
    assert f % n_chunks == 0 and w % LANES == 0 and (2 * w) % MXU_COLS == 0 and d % rows == 0
    blk = lambda m: pl.BlockSpec((None, None, rows, m), lambda l, h, r: (l, h, r, 0))
    return pl.pallas_call(
        functools.partial(_gate_up_kernel, n_chunks=n_chunks),
        grid=(n_l, n_h, d // rows),
        in_specs=[blk(f), blk(f)],
        out_specs=blk(2 * f),
        out_shape=jax.ShapeDtypeStruct((n_l, n_h, d, 2 * f), BF16),
        compiler_params=_params(3),
        name="ffn_gate_up_bf16",
    )(wg, wu)


def _ffn_post(x, wgu, wd, g, b, *, layer, half, alpha, tm, sub, n_chunks):
    n, d = x.shape
    f = wd.shape[-2]
    assert tm % sub == 0
    row = pl.BlockSpec((tm, d), lambda i: (i, 0))

    def weight(k, m):
        return pl.BlockSpec((None, None, k, m), lambda i: (layer, half, 0, 0), pipeline_mode=pl.Buffered(1))

    return pl.pallas_call(
        functools.partial(_ffn_kernel, alpha=alpha, n_chunks=n_chunks, sub=sub),
        grid=(n // tm,),
        in_specs=[row, weight(d, 2 * f), weight(f, d), _const_spec((1, d)), _const_spec((1, d))],
        out_specs=row,
        out_shape=jax.ShapeDtypeStruct((n, d), F32),
        compiler_params=_params(1),
        name="ffn_post",
    )(x, wgu, wd, g, b)


def _mla_proj_kernel(x_ref, win_ref, qn_ref, kvn_ref, wuq_ref, wukt_ref, inv_ref, sgn_ref, *more,
                     tm, sub, tk, pos0, seq_len, q_rank, kv_rank, rope_dim, nope_dim, scale, n_alias, n_stack):
    qcat_ref, ckv_ref, kr_ref, *rest = more[n_alias:]
    key_refs, rope_sc = (rest[:2], rest[2:]) if tk else ((), rest)
    heads = qcat_ref.shape[0]

    def put(ref, rows, val):
        if n_stack:
            ref[0, rows, :] = val
            for other in range(1, n_stack):
                ref[other, rows, :] = jnp.zeros_like(val)
        else:
            ref[rows, :] = val
    o_r = q_rank + kv_rank
    n_nope = heads * nope_dim
    n_rope = heads * rope_dim

    if seq_len == 1:
        ang = jnp.full((1, 1), pos0, F32) * inv_ref[...]
        cos_t = jnp.cos(ang)
        sin_t = jnp.sin(ang) * sgn_ref[...]
    else:
        cos_off, sin_off = rope_sc

        @pl.when(pl.program_id(0) == 0)
        def _():
            off = lax.broadcasted_iota(jnp.int32, (tm, 1), 0).astype(F32) * inv_ref[...]
            cos_off[...] = jnp.cos(off)
            sin_off[...] = jnp.sin(off)

        start = pos0 + (pl.program_id(0) * tm) % seq_len
        ang0 = jnp.full((1, 1), start, jnp.int32).astype(F32) * inv_ref[...]
        cos0, sin0 = jnp.cos(ang0), jnp.sin(ang0)

    for r in range(tm // sub):
        rows = slice(r * sub, (r + 1) * sub)
        if seq_len == 1:
            cos2, sin2 = cos_t, sin_t
        else:
            cos2 = cos0 * cos_off[rows, :] - sin0 * sin_off[rows, :]
            sin2 = (sin0 * cos_off[rows, :] + cos0 * sin_off[rows, :]) * sgn_ref[...]
        lat = _dot(x_ref[rows, :].astype(BF16), win_ref[...])
        c_q = _rmsnorm(lat[:, :q_rank], qn_ref[...] * scale)
        c_kv = _rmsnorm(lat[:, q_rank:o_r], kvn_ref[...])
        kr_raw = lat[:, o_r:o_r + rope_dim]
        kr_swp = lat[:, o_r + LANES:o_r + LANES + rope_dim]
        k_rope = kr_raw * cos2[:, :rope_dim] + kr_swp * sin2[:, :rope_dim]
        put(ckv_ref, rows, c_kv)
        put(kr_ref, rows, k_rope)
        if tk:
            kcat_ref, vt_ref = key_refs
            kcat_ref[rows, :kv_rank] = c_kv.astype(BF16)
            kcat_ref[rows, kv_rank:] = k_rope.astype(BF16)
            for blk in range(sub // tk):
                vt_ref[r * (sub // tk) + blk] = c_kv[blk * tk:(blk + 1) * tk, :].T.astype(BF16)

        q = _dot(c_q.astype(BF16), wuq_ref[...])
        reps = n_rope // LANES
        cos_h = jnp.concatenate([cos2] * reps, axis=-1)
        sin_h = jnp.concatenate([sin2] * reps, axis=-1)
        q_rope = q[:, n_nope:n_nope + n_rope] * cos_h + q[:, n_nope + n_rope:] * sin_h
        for h in range(heads):
            q_nope = q[:, h * nope_dim:(h + 1) * nope_dim].astype(BF16)
            qcat_ref[h, rows, :kv_rank] = _dot(q_nope, wukt_ref[h]).astype(BF16)
            qcat_ref[h, rows, kv_rank:] = q_rope[:, h * rope_dim:(h + 1) * rope_dim].astype(BF16)


def _mla_weights(w_in, w_uq, w_uk, w_uv, w_o, *, q_rank, kv_rank, rope_dim, heads, nope_dim):
    half = rope_dim // 2
    d = w_in.shape[0]

    def swap(w):
        return jnp.concatenate([w[..., half:], w[..., :half]], axis=-1)

    w_r = w_in[:, q_rank + kv_rank:]
    zpad = jnp.zeros((d, LANES - rope_dim), w_in.dtype)
    w_in_x = jnp.concatenate([w_in[:, :q_rank + kv_rank], w_r, zpad, swap(w_r), zpad], axis=1)
    uq = w_uq.reshape(q_rank, heads, nope_dim + rope_dim)
    uq_n = uq[:, :, :nope_dim].reshape(q_rank, heads * nope_dim)
    uq_r = uq[:, :, nope_dim:]
    w_uq_x = jnp.concatenate([uq_n, uq_r.reshape(q_rank, heads * rope_dim),
                              swap(uq_r).reshape(q_rank, heads * rope_dim)], axis=1)
    w_uk_t = jnp.transpose(w_uk, (1, 2, 0))
    w_uv_h = jnp.transpose(w_uv, (1, 0, 2))
    return (w_in_x.astype(BF16), w_uq_x.astype(BF16), w_uk_t.astype(BF16),
            w_uv_h.astype(BF16), w_o.astype(BF16))


def _rope_tables(rope_dim):
    half = rope_dim // 2
    inv = ROPE_THETA ** (-jnp.arange(half, dtype=F32) / half)
    reps = LANES // half
    inv_l = jnp.tile(inv, reps)[None, :]
    sgn = jnp.tile(jnp.concatenate([-jnp.ones((half,), F32), jnp.ones((half,), F32)]), reps // 2)[None, :]
    return inv_l, sgn


def _mla_project(x, w_in_x, q_norm, kv_norm, w_uq_x, w_uk_t, *, tm, tk, pos0, seq_len, rope_dim, scale,
                 stack=None):
    n, d = x.shape
    heads, nope_dim, kv_rank = w_uk_t.shape
    q_rank = w_uq_x.shape[0]
    inv_l, sgn = _rope_tables(rope_dim)
    row = lambda w: pl.BlockSpec((tm, w), lambda i: (i, 0))
    hrow = lambda w: pl.BlockSpec((heads, tm, w), lambda i: (0, i, 0))
    sub = _row_tile(tm, max(ROW_SUB, tk))
    out_specs = [hrow(kv_rank + rope_dim), row(kv_rank), row(rope_dim)]
    out_shape = [jax.ShapeDtypeStruct((heads, n, kv_rank + rope_dim), BF16),
                 jax.ShapeDtypeStruct((n, kv_rank), F32),
                 jax.ShapeDtypeStruct((n, rope_dim), F32)]
    extra_in, extra_specs, aliases, n_stack = [], [], {}, 0
    if stack:
        n_layers, layer, previous = stack
        out_shape[1:3] = [jax.ShapeDtypeStruct((n_layers, n, kv_rank), F32),
                          jax.ShapeDtypeStruct((n_layers, n, rope_dim), F32)]
        if layer == 0:
            n_stack = n_layers
            srow = lambda w: pl.BlockSpec((n_layers, tm, w), lambda i: (0, i, 0))
        else:
            srow = lambda w: pl.BlockSpec((None, tm, w), lambda i: (layer, i, 0))
            extra_in = list(previous)
            extra_specs = [pl.BlockSpec(memory_space=pl.ANY)] * 2
            aliases = {8: 1, 9: 2}
        out_specs[1:3] = [srow(kv_rank), srow(rope_dim)]
    kern = functools.partial(
        _mla_proj_kernel, tm=tm, sub=sub, tk=tk, pos0=pos0, seq_len=seq_len, q_rank=q_rank, kv_rank=kv_rank,
        rope_dim=rope_dim, nope_dim=nope_dim, scale=scale, n_alias=len(extra_in), n_stack=n_stack)
    assert seq_len == 1 or seq_len % tm == 0, "a row tile must not straddle two sequences"
    if tk:
        assert sub % tk == 0
        out_specs +=[row(kv_rank + rope_dim), pl.BlockSpec((tm // tk, kv_rank, tk), lambda i: (i, 0, 0))]
        out_shape += [jax.ShapeDtypeStruct((n, kv_rank + rope_dim), BF16),
                      jax.ShapeDtypeStruct((n // tk, kv_rank, tk), BF16)]
    return pl.pallas_call(
        kern,
        grid=(n // tm,),
        in_specs=[row(d), _const_spec(w_in_x.shape), _const_spec((1, q_rank)), _const_spec((1, kv_rank)),
                  _const_spec(w_uq_x.shape), _const_spec(w_uk_t.shape),
                  _const_spec((1, LANES)), _const_spec((1, LANES))] + extra_specs,
        input_output_aliases=aliases,
        out_specs=out_specs,
        out_shape=out_shape,
        scratch_shapes=[] if seq_len == 1 else [pltpu.VMEM((tm, LANES), F32), pltpu.VMEM((tm, LANES), F32)],
        compiler_params=_params(1),
        name="mla_project",
    )(x, w_in_x, q_norm[None, :], kv_norm[None, :], w_uq_x, w_uk_t, inv_l, sgn, *extra_in)


def _prompt_attn_kernel(q_ref, k_ref, vt_ref, o_ref, m_sc, l_sc, acc_sc, *, tq, tk):
    heads, _, ck = q_ref.shape
    c = vt_ref.shape[1]
    qi = pl.program_id(1)
    q = q_ref[...].reshape(heads * tq, ck)

    def scores(j, n_blk):
        k0 = pl.multiple_of(j * tk, tk)
        return k0, _dot_nt(k_ref[pl.ds(k0, n_blk * tk), :], q)

    def weighted_values(j, n_blk, p):
        pv = _dot(vt_ref[j], p[:tk])
        for i in range(1, n_blk):
            pv = pv + _dot(vt_ref[j + i], p[i * tk:(i + 1) * tk])
        return pv

    n_full = (qi * tq) // tk
    k0, s = scores(n_full, 1)
    k_pos = k0 + lax.broadcasted_iota(jnp.int32, s.shape, 0)
    q_pos = qi * tq + (lax.broadcasted_iota(jnp.int32, s.shape, 1) & (tq - 1))
    s = jnp.where(k_pos <= q_pos, s, NEG_INF)
    m0 = jnp.max(s, axis=0, keepdims=True)
    p = jnp.exp2(s - m0)
    m_sc[...] = m0
    l_sc[...] = jnp.sum(p, axis=0, keepdims=True)
    acc_sc[...] = weighted_values(n_full, 1, p.astype(BF16))

    def step(j, n_blk):
        _, s = scores(j, n_blk)
        m_prev = m_sc[...]
        m_new = jnp.maximum(m_prev, jnp.max(s, axis=0, keepdims=True))
        a = jnp.exp2(m_prev - m_new)
        p = jnp.exp2(s - m_new)
        l_sc[...] = a * l_sc[...] + jnp.sum(p, axis=0, keepdims=True)
        acc_sc[...] = a * acc_sc[...] + weighted_values(j, n_blk, p.astype(BF16))
        m_sc[...] = m_new

    def body(jj, carry):
        step(ATTN_WIDE * jj, ATTN_WIDE)
        return carry

    lax.fori_loop(0, n_full // ATTN_WIDE, body, 0)

    def rest(j, carry):
        step(j, 1)
        return carry

    lax.fori_loop(n_full - n_full % ATTN_WIDE, n_full, rest, 0)

    out = (acc_sc[...] * (1.0 / l_sc[...])).T
    for h in range(heads):
        o_ref[:, h * c:(h + 1) * c] = out[h * tq:(h + 1) * tq, :].astype(o_ref.dtype)


def _prompt_attention(q, k, vt, *, batch, seq, tq, tk):
    heads, n, ck = q.shape
    c = vt.shape[1]
    assert tq & (tq - 1) == 0 and tk % tq == 0 and seq % tk == 0 and vt.shape == (n // tk, c, tk)
    nq = seq // tq
    nk = seq // tk
    return pl.pallas_call(
        functools.partial(_prompt_attn_kernel, tq=tq, tk=tk),
        grid=(batch, nq),
        in_specs=[pl.BlockSpec((heads, tq, ck), lambda b, i: (0, b * nq + i, 0)),
                  pl.BlockSpec((seq, ck), lambda b, i: (b, 0)),
                  pl.BlockSpec((nk, c, tk), lambda b, i: (b, 0, 0))],
        out_specs=pl.BlockSpec((tq, heads * c), lambda b, i: (b * nq + i, 0)),
        out_shape=jax.ShapeDtypeStruct((n, heads * c), BF16),
        scratch_shapes=[pltpu.VMEM((1, heads * tq), F32), pltpu.VMEM((1, heads * tq), F32),
                        pltpu.VMEM((c, heads * tq), F32)],
        compiler_params=_params(2),
        name="prompt_attention",
    )(q, k, vt)


def _sample_attn_kernel(pt_ref, q_ref, cnew_ref, rnew_ref, ckv_hbm, krt_hbm, o_ref,
                        kbuf, rbuf, sem, *, layer, n_pages, page, n_chunks):
    b = pl.program_id(0)
    nb = pl.num_programs(0)
    slot = b % 2

    def page_copies(row, dst_slot, p):
        pg = pt_ref[row * n_pages + p]
        keys = pl.ds(p * page, page)
        return (pltpu.make_async_copy(ckv_hbm.at[layer, pg], kbuf.at[dst_slot, keys], sem.at[dst_slot, 0]),
                pltpu.make_async_copy(krt_hbm.at[layer, pg], rbuf.at[dst_slot, :, keys], sem.at[dst_slot, 1]))

    def start_row(row, dst_slot):
        for p in range(n_pages):
            for cp in page_copies(row, dst_slot, p):
                cp.start()

    @pl.when(b == 0)
    def _():
        start_row(0, 0)

    @pl.when(b + 1 < nb)
    def _():
        start_row(b + 1, 1 - slot)

    for p in range(n_pages):
        for cp in page_copies(b, slot, p):
            cp.wait()

    ql = q_ref[0, :, :kbuf.shape[2]]
    qr = q_ref[0, :, kbuf.shape[2]:]
    c_new = cnew_ref[0]
    r_new = rnew_ref[0]
    m = (jnp.sum(ql.astype(F32) * c_new, axis=-1, keepdims=True)
         + jnp.sum(qr.astype(F32) * r_new, axis=-1, keepdims=True))
    l = jnp.ones_like(m)
    acc = jnp.broadcast_to(c_new, (ql.shape[0], c_new.shape[1]))
    ck = (n_pages * page) // n_chunks
    for c in range(n_chunks):
        keys = pl.ds(c * ck, ck)
        kc = kbuf[slot, keys, :].astype(BF16)
        s = _dot_nt(ql, kc) + _dot(qr, rbuf[slot, :, keys].astype(BF16))
        m_new = jnp.maximum(m, jnp.max(s, axis=-1, keepdims=True))
        a = jnp.exp2(m - m_new)
        p = jnp.exp2(s - m_new)
        l = a * l + jnp.sum(p, axis=-1, keepdims=True)
        acc = a * acc + _dot(p.astype(BF16), kc)
        m = m_new
    o_ref[0] = (acc / l).astype(o_ref.dtype)


def _sample_attention(page_table, q, c_new, r_new, cache_kv, cache_kr_t, *, layer):
    db, heads, _ = q.shape
    c, r = c_new.shape[2], r_new.shape[2]
    assert q.shape[2] == c + r
    n_pages = page_table.shape[1]
    page = cache_kv.shape[2]
    assert cache_kr_t.shape[2:] == (r, page)
    past = n_pages * page
    n_chunks = max(1, past // DECODE_CHUNK)
    blk = lambda h, w: pl.BlockSpec((1, h, w), lambda b, pt: (b, 0, 0))
    grid_spec = pltpu.PrefetchScalarGridSpec(
        num_scalar_prefetch=1,
        grid=(db,),
        in_specs=[blk(heads, c + r), blk(1, c), blk(1, r),
                  pl.BlockSpec(memory_space=pl.ANY), pl.BlockSpec(memory_space=pl.ANY)],
        out_specs=blk(heads, c),
        scratch_shapes=[pltpu.VMEM((2, past, c), F32), pltpu.VMEM((2, r, past), F32),
                        pltpu.SemaphoreType.DMA((2, 2))],
    )
    return pl.pallas_call(
        functools.partial(_sample_attn_kernel, layer=layer, n_pages=n_pages, page=page, n_chunks=n_chunks),
        grid_spec=grid_spec,
        out_shape=jax.ShapeDtypeStruct((db, heads, c), BF16),
        compiler_params=_params(1),
        name="sample_attention",
    )(page_table.reshape(-1), q, c_new, r_new, cache_kv, cache_kr_t)


def _mla_out_kernel(o_ref, x_ref, wuv_ref, wo_ref, g_ref, b_ref, y_ref, *, alpha, sub):
    heads, c, _ = wuv_ref.shape
    for r in range(x_ref.shape[0] // sub):
        rows = slice(r * sub, (r + 1) * sub)
        parts = [_dot(o_ref[rows, h * c:(h + 1) * c], wuv_ref[h]).astype(BF16) for h in range(heads)]
        mix = _dot(jnp.concatenate(parts, axis=-1), wo_ref[...])
        y_ref[rows, :] = _layernorm(alpha * x_ref[rows, :] + mix, g_ref[...], b_ref[...])


def _mla_out_post(o_lat, x, w_uv_h, w_o, g, b, *, alpha, tm, sub):
    n, d = x.shape
    assert tm % sub == 0
    row = lambda w: pl.BlockSpec((tm, w), lambda i: (i, 0))
    return pl.pallas_call(
        functools.partial(_mla_out_kernel, alpha=alpha, sub=sub),
        grid=(n // tm,),
        in_specs=[row(o_lat.shape[1]), row(d), _const_spec(w_uv_h.shape), _const_spec(w_o.shape),
                  _const_spec((1, d)), _const_spec((1, d))],
        out_specs=row(d),
        out_shape=jax.ShapeDtypeStruct((n, d), F32),
        compiler_params=_params(1),
        name="mla_out_post",
    )(o_lat, x, w_uv_h, w_o, g, b)


def _cmlp_kernel(*refs, alpha, sub, heads, sample):
    if sample:
        (x_ref, win_ref, bin_ref, vg_ref, vb_ref, wcol_ref, bcol_ref, wout_ref, g_ref, b_ref,
         y_ref, v_ref) = refs
    else:
        x_ref, win_ref, bin_ref, vg_ref, vb_ref, ws_ref, bs_ref, wout_ref, g_ref, b_ref, y_ref = refs
        ri = lax.broadcasted_iota(jnp.int32, (CHUNK, CHUNK), 0)
        ci = lax.broadcasted_iota(jnp.int32, (CHUNK, CHUNK), 1)
        w_tril = [jnp.where(ci <= ri, ws_ref[h], 0.0).astype(BF16) for h in range(heads)]
    dc = wout_ref.shape[0]
    hd = dc // heads
    group = max(k for k in range(1, heads + 1)
                if heads % k == 0 and (k * hd) % MXU_COLS == 0 and k * hd <= CMLP_GROUP_COLS)
    for r in range(x_ref.shape[0] // sub):
        rows = slice(r * sub, (r + 1) * sub)
        x = x_ref[rows, :]
        xb = x.astype(BF16)
        v = _gelu_tanh(_dot(xb, win_ref[:, dc:]) + bin_ref[:, dc:])
        vn = _layernorm(v, vg_ref[...], vb_ref[...])
        if sample:
            v_ref[rows, :] = vn
            mixed = vn * wcol_ref[...] + bcol_ref[...]
            u = _gelu_tanh(_dot(xb, win_ref[:, :dc]) + bin_ref[:, :dc])
            mix = _dot((u * mixed).astype(BF16), wout_ref[...])
        else:
            vn_b = vn.astype(BF16)
            mix = None
            for h0 in range(0, heads, group):
                parts = []
                for h in range(h0, h0 + group):
                    cols = slice(h * hd, (h + 1) * hd)
                    bias = bs_ref[h]
                    parts.append(jnp.concatenate(
                        [_dot(w_tril[h], vn_b[c * CHUNK:(c + 1) * CHUNK, cols]) + bias
                         for c in range(sub // CHUNK)], axis=0))
                mixed = jnp.concatenate(parts, axis=-1)
                gcols = slice(h0 * hd, (h0 + group) * hd)
                u = _gelu_tanh(_dot(xb, win_ref[:, gcols]) + bin_ref[:, gcols])
                d = _dot((u * mixed).astype(BF16), wout_ref[gcols, :])
                mix = d if mix is None else mix + d
        y_ref[rows, :] = _layernorm(alpha * x + mix, g_ref[...], b_ref[...])


def _chunk_mlp_post(x, w_in, b_in, v_g, v_b, w_s, b_s, w_out, g, b, *, alpha, tm, sub, sample):
    n, d = x.shape
    dc = w_out.shape[0]
    heads = w_s.shape[0]
    assert tm % sub == 0
    row = pl.BlockSpec((tm, d), lambda i: (i, 0))
    common = [row, _const_spec(w_in.shape), _const_spec((1, 2 * dc)), _const_spec((1, dc)), _const_spec((1, dc))]
    tail = [_const_spec(w_out.shape), _const_spec((1, d)), _const_spec((1, d))]
    kern = functools.partial(_cmlp_kernel, alpha=alpha, sub=sub, heads=heads, sample=sample)
    y_shape = jax.ShapeDtypeStruct((n, d), F32)
    if sample:
        hd = dc // heads
        wcol = jnp.repeat(w_s[:, 0, 0], hd)[None, :]
        bcol = jnp.repeat(b_s[:, 0], hd)[None, :]
        return pl.pallas_call(
            kern, grid=(n // tm,),
            in_specs=common + [_const_spec((1, dc)), _const_spec((1, dc))] + tail,
            out_specs=[row, pl.BlockSpec((tm, dc), lambda i: (i, 0))],
            out_shape=[y_shape, jax.ShapeDtypeStruct((n, dc), F32)],
            compiler_params=_params(1), name="chunk_mlp_sample",
        )(x, w_in, b_in[None, :], v_g[None, :], v_b[None, :], wcol, bcol, w_out, g, b)
    assert sub % CHUNK == 0
    y = pl.pallas_call(
        kern, grid=(n // tm,),
        in_specs=common + [_const_spec(w_s.shape), _const_spec((heads, CHUNK, 1))] + tail,
        out_specs=row, out_shape=y_shape,
        compiler_params=_params(1), name="chunk_mlp_prompt",
    )(x, w_in, b_in[None, :], v_g[None, :], v_b[None, :], w_s, b_s[:, :, None], w_out, g, b)
    return y, None


def _pool_tail(pooled_groups, x, wgrp_ref, scale_ref, wout_ref, g_ref, b_ref, alpha):
    z = jnp.concatenate(
        [_dot(p.astype(BF16), wgrp_ref[gi]) for gi, p in enumerate(pooled_groups)], axis=-1)
    mix = _dot((z * scale_ref[...]).astype(BF16), wout_ref[...])
    return _layernorm(alpha * x + mix, g_ref[...], b_ref[...])


def _pool_prompt_kernel(x_ref, halo_ref, win_ref, wgrp_ref, scale_ref, wout_ref, g_ref, b_ref,
                        y_ref, last_ref, h_sc, *, alpha, tm, sub, tiles_per_seq, halo):
    t0 = (pl.program_id(0) % tiles_per_seq) * tm
    h = _dot(x_ref[...].astype(BF16), win_ref[...])
    h_halo = _dot(halo_ref[...].astype(BF16), win_ref[...])
    h_sc[:halo, :] = jnp.where(t0 > 0, h_halo, 0.0)
    h_sc[halo:, :] = h
    last_ref[0] = h[tm - halo:, :]
    gd = h.shape[1] // len(POOL_WINDOWS)
    for r in range(tm // sub):
        rows = slice(r * sub, (r + 1) * sub)
        t = t0 + r * sub + lax.broadcasted_iota(jnp.int32, (sub, 1), 0)
        pooled = []
        for gi, w in enumerate(POOL_WINDOWS):
            cols = slice(gi * gd, (gi + 1) * gd)
            s = h_sc[r * sub:(r + 1) * sub + halo, cols]
            span = 1
            while span < w:
                s = s + pltpu.roll(s, span, axis=0)
                span *= 2
            inv_cnt = 1.0 / jnp.minimum(t + 1, w).astype(F32)
            pooled.append(s[halo:, :] * inv_cnt - h_sc[halo + r * sub:halo + (r + 1) * sub, cols])
        y_ref[rows, :] = _pool_tail(pooled, x_ref[rows, :], wgrp_ref, scale_ref, wout_ref, g_ref, b_ref, alpha)


def _pool_prompt_post(x, w_in, w_grp, scale, w_out, g, b, *, alpha, tm, seq):
    n, d = x.shape
    dp = w_in.shape[1]
    halo = max(POOL_WINDOWS)
    assert seq % tm == 0 and tm % halo == 0 and halo % 8 == 0
    assert all(w & (w - 1) == 0 for w in POOL_WINDOWS), "window sums are built by doubling"
    tiles_per_seq = seq // tm
    hb = tm // halo
    row = pl.BlockSpec((tm, d), lambda i: (i, 0))
    kern = functools.partial(_pool_prompt_kernel, alpha=alpha, tm=tm, sub=_row_tile(tm, ROW_SUB),
                             tiles_per_seq=tiles_per_seq, halo=halo)
    return pl.pallas_call(
        kern, grid=(n // tm,),
        in_specs=[row, pl.BlockSpec((halo, d), lambda i: (jnp.maximum(i * hb - 1, 0), 0)),
                  _const_spec(w_in.shape), _const_spec(w_grp.shape), _const_spec((1, dp)),
                  _const_spec(w_out.shape), _const_spec((1, d)), _const_spec((1, d))],
        out_specs=[row, pl.BlockSpec((1, halo, dp), lambda i: (i // tiles_per_seq, 0, 0))],
        out_shape=[jax.ShapeDtypeStruct((n, d), F32),
                   jax.ShapeDtypeStruct((n // seq, halo, dp), F32)],
        scratch_shapes=[pltpu.VMEM((tm + halo, dp), F32)],
        compiler_params=_params(1), name="pool_prompt",
    )(x, x, w_in, w_grp, scale[None, :], w_out, g, b)


def _pool_sample_kernel(x_ref, st_ref, win_ref, wgrp_ref, scale_ref, wout_ref, g_ref, b_ref,
                        y_ref, h_ref, *, alpha):
    x = x_ref[...]
    h = _dot(x.astype(BF16), win_ref[...])
    h_ref[...] = h
    nbuf = st_ref.shape[0]
    gd = h.shape[1] // len(POOL_WINDOWS)
    pooled = []
    for gi, w in enumerate(POOL_WINDOWS):
        cols = slice(gi * gd, (gi + 1) * gd)
        tot = h[:, cols]
        for k in range(1, w):
            tot = tot + st_ref[nbuf - k, :, cols]
        pooled.append(tot / float(w) - h[:, cols])
    y_ref[...] = _pool_tail(pooled, x, wgrp_ref, scale_ref, wout_ref, g_ref, b_ref, alpha)


def _pool_sample_post(x, state, w_in, w_grp, scale, w_out, g, b, *, alpha):
    n, d = x.shape
    dp = w_in.shape[1]
    assert state.shape[0] == max(POOL_WINDOWS) - 1
    return pl.pallas_call(
        functools.partial(_pool_sample_kernel, alpha=alpha), grid=(1,),
        in_specs=[_const_spec((n, d)), _const_spec(state.shape), _const_spec(w_in.shape),
                  _const_spec(w_grp.shape), _const_spec((1, dp)), _const_spec(w_out.shape),
                  _const_spec((1, d)), _const_spec((1, d))],
        out_specs=[pl.BlockSpec((n, d), lambda i: (0, 0)), pl.BlockSpec((n, dp), lambda i: (0, 0))],
        out_shape=[jax.ShapeDtypeStruct((n, d), F32), jax.ShapeDtypeStruct((n, dp), F32)],
        compiler_params=_params(1), name="pool_sample",
    )(x, state, w_in, w_grp, scale[None, :], w_out, g, b)


def _row_tile(n, want):
    tm = min(n, want)
    assert n % tm == 0
    return tm


def kernel(x_prompt, x_sample, cache_kv_latent, cache_k_rope, state_pool, page_table, ln_g, ln_b, ffn_w_gate, ffn_w_up, ffn_w_down, a_w_in, a_q_norm, a_kv_norm, a_w_uq, a_w_uk, a_w_uv, a_w_o, b_w_in, b_b_in, b_v_norm_g, b_v_norm_b, b_w_s, b_b_s, b_w_out, c_w_in, c_w_grp, c_scale, c_w_out):
    batch, seq, d = x_prompt.shape
    db, t_new, _ = x_sample.shape
    assert t_new == 1, "decode attention handles one new token per sample row"
    depth = ln_g.shape[0]
    alpha = (2 * depth) ** 0.25
    kv_rank, heads, nope_dim = a_w_uk.shape[1:]
    rope_dim = cache_k_rope.shape[-1]
    q_rank = a_w_uq.shape[1]
    scale = (nope_dim + rope_dim) ** -0.5 * math.log2(math.e)
    past_len = page_table.shape[1] * cache_kv_latent.shape[2]
    n_p, n_s = batch * seq, db * t_new
    tm_p, tm_s = _row_tile(n_p, SEQ_TILE), _row_tile(n_s, SEQ_TILE)

    def sub_tiled(y):
        tm = _row_tile(y.shape[0], ROW_TILE)
        return dict(tm=tm, sub=_row_tile(tm, ROW_SUB))

    y_p = x_prompt.reshape(n_p, d)
    y_s = x_sample.reshape(n_s, d)
    wgu, wd = _ffn_gate_up(ffn_w_gate, ffn_w_up, FFN_CHUNKS), ffn_w_down.astype(BF16)
    cache_kr_t = jnp.swapaxes(cache_k_rope, 2, 3)
    n_mla, latents_p = a_w_in.shape[0], None
    outs = {k: [] for k in ("ckv_s", "kr_s", "v_s", "pool_p", "pool_s")}

    def ffn(y, i, k, ln_idx):
        return _ffn_post(y, wgu, wd, ln_g[i, ln_idx][None, :], ln_b[i, ln_idx][None, :], layer=i, half=k,
                         alpha=alpha, n_chunks=FFN_CHUNKS, **sub_tiled(y))

    for i in range(depth):
        kind, j = i % 3, i // 3
        y_p = ffn(y_p, i, 0, 0)
        y_s = ffn(y_s, i, 0, 0)
        g1, b1 = ln_g[i, 1][None, :], ln_b[i, 1][None, :]

        if kind == 0:
            w_in_x, w_uq_x, w_uk_t, w_uv_h, w_o = _mla_weights(
                a_w_in[j], a_w_uq[j], a_w_uk[j], a_w_uv[j], a_w_o[j],
                q_rank=q_rank, kv_rank=kv_rank, rope_dim=rope_dim, heads=heads, nope_dim=nope_dim)
            proj = functools.partial(_mla_project, w_in_x=w_in_x, q_norm=a_q_norm[j], kv_norm=a_kv_norm[j],
                                     w_uq_x=w_uq_x, w_uk_t=w_uk_t, rope_dim=rope_dim, scale=scale)
            q, ckv, kr, k_cat, vt = proj(y_p, tm=tm_p, tk=ATTN_TK, pos0=0, seq_len=seq,
                                         stack=(n_mla, j, latents_p))
            latents_p = (ckv, kr)
            o_lat = _prompt_attention(q, k_cat, vt, batch=batch, seq=seq, tq=ATTN_TQ, tk=ATTN_TK)
            y_p = _mla_out_post(o_lat, y_p, w_uv_h, w_o, g1, b1, alpha=alpha, **sub_tiled(y_p))

            q, ckv, kr = proj(y_s, tm=tm_s, tk=0, pos0=past_len, seq_len=t_new)
            o_s = _sample_attention(page_table, jnp.transpose(q, (1, 0, 2)),
                                    ckv[:, None, :], kr[:, None, :], cache_kv_latent, cache_kr_t, layer=j)
            y_s = _mla_out_post(o_s.reshape(n_s, heads * kv_rank), y_s, w_uv_h, w_o, g1, b1, alpha=alpha,
                                **sub_tiled(y_s))
            outs["ckv_s"].append(ckv.reshape(db, t_new, kv_rank))
            outs["kr_s"].append(kr.reshape(db, t_new, rope_dim))
        elif kind == 1:
            cm = functools.partial(_chunk_mlp_post, w_in=b_w_in[j].astype(BF16), b_in=b_b_in[j], v_g=b_v_norm_g[j],
                                   v_b=b_v_norm_b[j], w_s=b_w_s[j], b_s=b_b_s[j], w_out=b_w_out[j].astype(BF16),
                                   g=g1, b=b1, alpha=alpha)
            y_p, _ = cm(y_p, sample=False, tm=tm_p, sub=_row_tile(tm_p, ROW_SUB))
            y_s, v_rows = cm(y_s, sample=True, **sub_tiled(y_s))
            outs["v_s"].append(v_rows.reshape(db, t_new, -1))
        else:
            w_in, w_grp, w_out = c_w_in[j].astype(BF16), c_w_grp[j].astype(BF16), c_w_out[j].astype(BF16)
            y_p, last = _pool_prompt_post(y_p, w_in, w_grp, c_scale[j], w_out, g1, b1, alpha=alpha, tm=tm_p, seq=seq)
            nbuf = state_pool.shape[2]
            outs["pool_p"].append(last[:, last.shape[1] - nbuf:, :])
            y_s, h_new = _pool_sample_post(y_s, jnp.transpose(state_pool[j], (1, 0, 2)), w_in, w_grp, c_scale[j], w_out, g1, b1, alpha=alpha)
            outs["pool_s"].append(jnp.concatenate([state_pool[j][:, 1:], h_new[:, None, :]], axis=1))

        y_p = ffn(y_p, i, 1, 2)
        y_s = ffn(y_s, i, 1, 2)

    return (y_p.reshape(batch, seq, d), y_s.reshape(db, t_new, d),
            latents_p[0].reshape(n_mla, batch, seq, kv_rank), latents_p[1].reshape(n_mla, batch, seq, rope_dim),
            jnp.stack(outs["ckv_s"]), jnp.stack(outs["kr_s"]),
            jnp.stack(outs["v_s"]), jnp.stack(outs["pool_p"]), jnp.stack(outs["pool_s"]))
```
